```python
import math
import jax, jax.numpy as jnp
from jax import lax
import numpy as np

D_MODEL = 4096
BATCH = 2
SEQ = 8192
DEPTH = 4

GRID_W = 64
CTX_LEN = 256
MIX_WIDTH = D_MODEL
HEAD_DIM = 128
NA_HEADS = (MIX_WIDTH // 2) // HEAD_DIM
NA_WIDTH = NA_HEADS * HEAD_DIM
HY_WIDTH = MIX_WIDTH - NA_WIDTH
WIN_H = 8
WIN_W = 16
HY_ORDER = 2
HY_SHORT = 3
HY_BANDS = 16
HY_EMB = 1 + 2 * HY_BANDS
HY_FILTER_HIDDEN = 64
HY_DECAY_TARGET = 1e-2
HY_FAST_DECAY = 0.3
HY_SLOW_DECAY = 1.5
GM_WIDTH = MIX_WIDTH
GM_GROUPS = GM_WIDTH // HEAD_DIM
CHUNK = 128
N_EXPERTS = 64
TOP_K = 8
EXPERT_DIM = 128
SHARED_DIM = TOP_K * EXPERT_DIM
ROUTED_SCALE = 2.5
ADA_RANK = 256
N_MOD = 6
EPS = 1e-6
N_EVEN = (DEPTH + 1) // 2
N_ODD = DEPTH // 2
EVEN_IN = 3 * NA_WIDTH + (HY_ORDER + 1) * HY_WIDTH
ODD_IN = 2 * GM_WIDTH

kernel_name = "hybrid_na_hyena_gmlp_moe_dit"


def rmsnorm(x, g):
    xf = x.astype(jnp.float32)
    y = xf * lax.rsqrt(jnp.mean(xf * xf, axis=-1, keepdims=True) + EPS)
    return (y * g.astype(jnp.float32)).astype(x.dtype)


def modulate(h, shift, scale):
    return h * (1.0 + scale) + shift


def ada_mod(cond, a, b, bias):
    m = (jax.nn.silu(cond) @ a) @ b + bias
    return m.reshape(cond.shape[:-1] + (N_MOD, D_MODEL))


def split_heads(t):
    return t.reshape(t.shape[:-1] + (NA_HEADS, HEAD_DIM))


def dense_attention(q, k, v):
    s = jnp.einsum('bqhd,bkhd->bhqk', q, k).astype(jnp.float32) * (HEAD_DIM ** -0.5)
    p = jax.nn.softmax(s, axis=-1).astype(v.dtype)
    return jnp.einsum('bhqk,bkhd->bqhd', p, v)


def neighborhood_attention(q, k, v, kc, vc, rpb):
    bsz, n, h, hd = q.shape
    rows = n // GRID_W
    kh = min(WIN_H, rows)
    scale = HEAD_DIM ** -0.5
    kg = k.reshape(bsz, rows, GRID_W, h, hd)
    vg = v.reshape(bsz, rows, GRID_W, h, hd)
    q_rows = jnp.moveaxis(q.reshape(bsz, rows, GRID_W, h, hd), 1, 0)
    r_idx = jnp.arange(rows, dtype=jnp.int32)
    r_start = jnp.clip(r_idx - kh // 2, 0, rows - kh)
    cols = jnp.arange(GRID_W, dtype=jnp.int32)
    c_start = jnp.clip(cols - WIN_W // 2, 0, GRID_W - WIN_W)
    c_idx = c_start[:, None] + jnp.arange(WIN_W, dtype=jnp.int32)
    col_off = c_idx - cols[:, None] + (WIN_W - 1)

    def one_row(args):
        q_row, r, rs = args
        k_band = lax.dynamic_slice_in_dim(kg, rs, kh, axis=1)
        v_band = lax.dynamic_slice_in_dim(vg, rs, kh, axis=1)
        k_win = k_band[:, :, c_idx]
        v_win = v_band[:, :, c_idx]
        row_off = rs + jnp.arange(kh, dtype=jnp.int32) - r + (WIN_H - 1)
        bias = rpb[:, row_off[None, :, None], col_off[:, None, :]]
        s_win = (jnp.einsum('bqhd,brqwhd->bhqrw', q_row, k_win).astype(jnp.float32) * scale
                 + bias.astype(jnp.float32))
        s_ctx = jnp.einsum('bqhd,bchd->bhqc', q_row, kc).astype(jnp.float32) * scale
        s = jnp.concatenate([s_win.reshape(bsz, h, GRID_W, kh * WIN_W), s_ctx], axis=-1)
        p = jax.nn.softmax(s, axis=-1).astype(v.dtype)
        p_win = p[..., :kh * WIN_W].reshape(bsz, h, GRID_W, kh, WIN_W)
        p_ctx = p[..., kh * WIN_W:]
        return (jnp.einsum('bhqrw,brqwhd->bqhd', p_win, v_win)
                + jnp.einsum('bhqc,bchd->bqhd', p_ctx, vc))

    out = lax.map(one_row, (q_rows, r_idx, r_start))
    return jnp.moveaxis(out, 0, 1).reshape(bsz, n, h, hd)


def short_conv(x, w, b):
    n = x.shape[1]
    pad = HY_SHORT // 2
    xp = jnp.pad(x, ((0, 0), (pad, HY_SHORT - 1 - pad), (0, 0)))
    out = b
    for i in range(HY_SHORT):
        out = out + xp[:, i:i + n] * w[i]
    return out


def hyena_filters(length, w1, b1, w2, b2, w3, b3, w4, freq):
    f32 = jnp.float32
    t_norm = jnp.linspace(0.0, 1.0, length, dtype=f32)[:, None]
    pos = jnp.arange(length, dtype=f32)[:, None]
    bands = jnp.linspace(1e-4, HY_BANDS - 1, HY_BANDS, dtype=f32)[None, :]
    ang = (2.0 * math.pi / length) * pos * bands
    feats = jnp.concatenate([t_norm, jnp.cos(ang), -jnp.sin(ang)], axis=-1)
    fr = freq.astype(f32)
    hdn = jnp.sin(fr[0] * (feats @ w1.astype(f32) + b1.astype(f32)))
    hdn = jnp.sin(fr[1] * (hdn @ w2.astype(f32) + b2.astype(f32)))
    hdn = jnp.sin(fr[2] * (hdn @ w3.astype(f32) + b3.astype(f32)))
    filt = (hdn @ w4.astype(f32)).reshape(length, 2, HY_ORDER, HY_WIDTH)
    deltas = jnp.abs(jnp.linspace(math.log(HY_DECAY_TARGET) / HY_FAST_DECAY,
                                  math.log(HY_DECAY_TARGET) / HY_SLOW_DECAY, HY_WIDTH, dtype=f32))
    filt = filt * jnp.exp(-t_norm * deltas)[:, None, None, :]
    fwd, bwd = filt[:, 0], filt[:, 1]
    kern = jnp.concatenate([fwd, jnp.zeros_like(fwd[:1]), bwd[:0:-1]], axis=0)
    return kern * lax.rsqrt(jnp.sum(kern * kern, axis=0, keepdims=True) + EPS)


def long_conv(z, kern, skip):
    n = z.shape[1]
    zf = z.astype(jnp.float32)
    zk = jnp.fft.rfft(zf, n=2 * n, axis=1) * jnp.fft.rfft(kern, axis=0)[None]
    y = jnp.fft.irfft(zk, n=2 * n, axis=1)[:, :n]
    return (y + zf * skip.astype(jnp.float32)).astype(z.dtype)


def hyena(p, conv_w, conv_b, w1, b1, w2, b2, w3, b3, w4, freq, skip):
    u = short_conv(p, conv_w, conv_b)
    parts = jnp.split(u, HY_ORDER + 1, axis=-1)
    gates, z = parts[:HY_ORDER], parts[HY_ORDER]
    kern = hyena_filters(p.shape[1], w1, b1, w2, b2, w3, b3, w4, freq)
    for o in range(HY_ORDER):
        z = gates[o] * long_conv(z, kern[:, o], skip[o])
    return z


def mixer_even(h_lat, h_ctx, w_in, q_gain, k_gain, rpb, hy_params, w_out, ctx_out):
    bsz, n, _ = h_lat.shape
    lc = h_ctx.shape[1]
    p_lat = h_lat @ w_in
    q, k, v = (split_heads(t) for t in jnp.split(p_lat[..., :3 * NA_WIDTH], 3, axis=-1))
    q, k = rmsnorm(q, q_gain), rmsnorm(k, k_gain)
    if ctx_out:
        p_ctx = h_ctx @ w_in
        qc, kc, vc = (split_heads(t) for t in jnp.split(p_ctx[..., :3 * NA_WIDTH], 3, axis=-1))
    else:
        kc, vc = (split_heads(t) for t in
                  jnp.split(h_ctx @ w_in[:, NA_WIDTH:3 * NA_WIDTH], 2, axis=-1))
    kc = rmsnorm(kc, k_gain)
    a_lat = neighborhood_attention(q, k, v, kc, vc, rpb).reshape(bsz, n, NA_WIDTH)
    o_lat = jnp.concatenate([a_lat, hyena(p_lat[..., 3 * NA_WIDTH:], *hy_params)], axis=-1) @ w_out
    if not ctx_out:
        return o_lat, None
    a_ctx = dense_attention(rmsnorm(qc, q_gain), kc, vc).reshape(bsz, lc, NA_WIDTH)
    o_ctx = jnp.concatenate([a_ctx, hyena(p_ctx[..., 3 * NA_WIDTH:], *hy_params)], axis=-1) @ w_out
    return o_lat, o_ctx


def gmlp(h, w_in, b_in, v_gain, w_s, b_s):
    bsz, n, _ = h.shape
    z = jax.nn.gelu(h @ w_in + b_in, approximate=False)
    u, v = jnp.split(z, 2, axis=-1)
    v = rmsnorm(v, v_gain)
    vb = v.reshape(bsz, n // CHUNK, CHUNK, GM_GROUPS, HEAD_DIM)
    s = jnp.einsum('gpq,bnqgc->bnpgc', w_s, vb) + b_s.T[:, :, None]
    return u * s.reshape(bsz, n, GM_WIDTH)


def moe(h, w_router, r_bias, w_gate, w_up, w_down, ws_gate, ws_up, ws_down):
    shp = h.shape
    t = h.reshape(-1, D_MODEL)
    scores = jax.nn.sigmoid((t @ w_router).astype(jnp.float32))
    _, idx = lax.top_k(scores + r_bias.astype(jnp.float32), TOP_K)
    sel = jnp.take_along_axis(scores, idx, axis=-1)
    g = sel / jnp.sum(sel, axis=-1, keepdims=True) * ROUTED_SCALE
    gates = jnp.einsum('tk,tke->te', g, jax.nn.one_hot(idx, N_EXPERTS, dtype=jnp.float32))
    hid = jax.nn.silu(jnp.einsum('td,edf->tef', t, w_gate)) * jnp.einsum('td,edf->tef', t, w_up)
    routed = jnp.einsum('tef,efd->td', hid * gates[..., None].astype(hid.dtype), w_down)
    shared = (jax.nn.silu(t @ ws_gate) * (t @ ws_up)) @ ws_down
    return (routed + shared).reshape(shp)


def setup_inputs(seed: int = 0) -> dict:
    key = jax.random.key(seed)
    ks = iter(jax.random.split(key, 64))

    def nrm(shape, scale):
        return jax.random.normal(next(ks), shape, jnp.float32) * scale

    def gain(shape):
        return 1.0 + nrm(shape, 0.05)

    d = D_MODEL
    fh = HY_FILTER_HIDDEN
    return {
        "x": nrm((BATCH, SEQ, d), 1.0),
        "c": nrm((BATCH, d), 1.0),
        "ctx": nrm((BATCH, CTX_LEN, d), 1.0),
        "c_ctx": nrm((d,), 1.0),
        "ada_a": nrm((DEPTH, d, ADA_RANK), d ** -0.5),
        "ada_b": nrm((DEPTH, ADA_RANK, N_MOD * d), 0.5 * ADA_RANK ** -0.5),
        "ada_bias": nrm((DEPTH, N_MOD * d), 0.02),
        "norm_mix": gain((DEPTH, d)),
        "norm_ffn": gain((DEPTH, d)),
        "even_w_in": nrm((N_EVEN, d, EVEN_IN), d ** -0.5),
        "na_q_gain": gain((N_EVEN, HEAD_DIM)),
        "na_k_gain": gain((N_EVEN, HEAD_DIM)),
        "na_rpb": nrm((N_EVEN, NA_HEADS, 2 * WIN_H - 1, 2 * WIN_W - 1), 0.1),
        "hy_conv_w": nrm((N_EVEN, HY_SHORT, (HY_ORDER + 1) * HY_WIDTH), HY_SHORT ** -0.5),
        "hy_conv_b": nrm((N_EVEN, (HY_ORDER + 1) * HY_WIDTH), 0.02),
        "hy_w1": nrm((N_EVEN, HY_EMB, fh), HY_EMB ** -0.5),
        "hy_b1": nrm((N_EVEN, fh), 0.1),
        "hy_w2": nrm((N_EVEN, fh, fh), fh ** -0.5),
        "hy_b2": nrm((N_EVEN, fh), 0.1),
        "hy_w3": nrm((N_EVEN, fh, fh), fh ** -0.5),
        "hy_b3": nrm((N_EVEN, fh), 0.1),
        "hy_w4": nrm((N_EVEN, fh, 2 * HY_ORDER * HY_WIDTH), fh ** -0.5),
        "hy_freq": 1.0 + nrm((N_EVEN, 3, fh), 0.1),
        "hy_skip": nrm((N_EVEN, HY_ORDER, HY_WIDTH), 0.5),
        "odd_w_in": nrm((N_ODD, d, ODD_IN), d ** -0.5),
        "odd_b_in": nrm((N_ODD, ODD_IN), 0.02),
        "gm_v_gain": gain((N_ODD, GM_WIDTH)),
        "gm_w_s": nrm((N_ODD, GM_GROUPS, CHUNK, CHUNK), CHUNK ** -0.5),
        "gm_b_s": 1.0 + nrm((N_ODD, GM_GROUPS, CHUNK), 0.02),
        "w_mix_out": nrm((DEPTH, MIX_WIDTH, d), MIX_WIDTH ** -0.5),
        "moe_router": nrm((DEPTH, d, N_EXPERTS), d ** -0.5),
        "moe_bias": nrm((DEPTH, N_EXPERTS), 0.01),
        "moe_w_gate": nrm((DEPTH, N_EXPERTS, d, EXPERT_DIM), d ** -0.5),
        "moe_w_up": nrm((DEPTH, N_EXPERTS, d, EXPERT_DIM), d ** -0.5),
        "moe_w_down": nrm((DEPTH, N_EXPERTS, EXPERT_DIM, d), EXPERT_DIM ** -0.5),
        "sh_w_gate": nrm((DEPTH, d, SHARED_DIM), d ** -0.5),
        "sh_w_up": nrm((DEPTH, d, SHARED_DIM), d ** -0.5),
        "sh_w_down": nrm((DEPTH, SHARED_DIM, d), SHARED_DIM ** -0.5),
    }


def reference(x, c, ctx, c_ctx, ada_a, ada_b, ada_bias, norm_mix, norm_ffn,
              even_w_in, na_q_gain, na_k_gain, na_rpb, hy_conv_w, hy_conv_b,
              hy_w1, hy_b1, hy_w2, hy_b2, hy_w3, hy_b3, hy_w4, hy_freq, hy_skip,
              odd_w_in, odd_b_in, gm_v_gain, gm_w_s, gm_b_s, w_mix_out,
              moe_router, moe_bias, moe_w_gate, moe_w_up, moe_w_down,
              sh_w_gate, sh_w_up, sh_w_down):
    lc = ctx.shape[1]
    lat, cx = x, ctx
    for layer in range(DEPTH):
        last = layer == DEPTH - 1
        is_even = layer % 2 == 0
        need_ctx = (not last) or is_even
        j = layer // 2
        m_lat = ada_mod(c, ada_a[layer], ada_b[layer], ada_bias[layer])[:, None]
        h_lat = modulate(rmsnorm(lat, norm_mix[layer]), m_lat[..., 0, :], m_lat[..., 1, :])
        if need_ctx:
            m_ctx = ada_mod(c_ctx, ada_a[layer], ada_b[layer], ada_bias[layer])[None, None]
            h_ctx = modulate(rmsnorm(cx, norm_mix[layer]), m_ctx[..., 0, :], m_ctx[..., 1, :])
        if is_even:
            hy_params = (hy_conv_w[j], hy_conv_b[j], hy_w1[j], hy_b1[j], hy_w2[j], hy_b2[j],
                         hy_w3[j], hy_b3[j], hy_w4[j], hy_freq[j], hy_skip[j])
            o_lat, o_ctx = mixer_even(h_lat, h_ctx, even_w_in[j], na_q_gain[j], na_k_gain[j],
                                      na_rpb[j], hy_params, w_mix_out[layer], not last)
        else:
            gm = (odd_w_in[j], odd_b_in[j], gm_v_gain[j], gm_w_s[j], gm_b_s[j])
            o_lat = gmlp(h_lat, *gm) @ w_mix_out[layer]
            o_ctx = None if last else gmlp(h_ctx, *gm) @ w_mix_out[layer]
        lat = lat + m_lat[..., 2, :] * o_lat
        moe_params = (moe_router[layer], moe_bias[layer], moe_w_gate[layer], moe_w_up[layer],
                      moe_w_down[layer], sh_w_gate[layer], sh_w_up[layer], sh_w_down[layer])
        if last:
            f_in = modulate(rmsnorm(lat, norm_ffn[layer]), m_lat[..., 3, :], m_lat[..., 4, :])
            lat = lat + m_lat[..., 5, :] * moe(f_in, *moe_params)
        else:
            cx = cx + m_ctx[..., 2, :] * o_ctx
            f_in = jnp.concatenate(
                [modulate(rmsnorm(cx, norm_ffn[layer]), m_ctx[..., 3, :], m_ctx[..., 4, :]),
                 modulate(rmsnorm(lat, norm_ffn[layer]), m_lat[..., 3, :], m_lat[..., 4, :])], axis=1)
            f = moe(f_in, *moe_params)
            cx = cx + m_ctx[..., 5, :] * f[:, :lc]
            lat = lat + m_lat[..., 5, :] * f[:, lc:]
    return lat
```

```python
import functools
import math

import numpy as np
import jax
import jax.numpy as jnp
from jax import lax
from jax.experimental import pallas as pl
from jax.experimental.pallas import tpu as pltpu

F32 = jnp.float32
BF16 = jnp.bfloat16

D_MODEL = 4096
BATCH = 2
SEQ = 8192
DEPTH = 4
GRID_W = 64
ROWS = SEQ // GRID_W
CTX_LEN = 256
HEAD_DIM = 128
NA_HEADS = 16
NA_WIDTH = NA_HEADS * HEAD_DIM
HY_WIDTH = D_MODEL - NA_WIDTH
WIN_H = 8
WIN_W = 16
HY_ORDER = 2
HY_SHORT = 3
HY_BANDS = 16
HY_DECAY_TARGET = 1e-2
HY_FAST_DECAY = 0.3
HY_SLOW_DECAY = 1.5
GM_WIDTH = D_MODEL
GM_GROUPS = GM_WIDTH // HEAD_DIM
CHUNK = 128
N_EXPERTS = 64
TOP_K = 8
EXPERT_DIM = 128
SHARED_DIM = TOP_K * EXPERT_DIM
ROUTED_SCALE = 2.5
N_MOD = 6
EPS = 1e-6

LANES = 128
VMEM_LIMIT = 56 * 1024 * 1024

TM = 1024
N_LAT = BATCH * SEQ
N_CTX = BATCH * CTX_LEN
T_REAL = N_LAT + N_CTX
T_PAD = -(-T_REAL // TM) * TM
CTX_ROW0 = N_LAT
TN_NORM = 512
MOE_HID = N_EXPERTS * EXPERT_DIM + SHARED_DIM
MOE_TN = 512
EXP_PER_TILE = MOE_TN // EXPERT_DIM
N_GATE_COLS = MOE_HID // EXPERT_DIM
MOE_TK = 2304
NEG = -1e30

NA_R = 4
NA_Q = NA_R * GRID_W
NA_KROWS = NA_R + WIN_H
NA_KTOK = NA_KROWS * GRID_W
NA_QBLOCKS = ROWS // NA_R


def _mod_row(tile_rows):
  def f(i):
    return jnp.minimum((i * tile_rows) // SEQ, BATCH)
  return f


def _cparams(sem):
  return pltpu.CompilerParams(dimension_semantics=sem, vmem_limit_bytes=VMEM_LIMIT)


def _ada_kernel(cond_ref, a_ref, b_ref, bias_ref, o_ref):
  cnd = cond_ref[...]
  sc = cnd * jax.nn.sigmoid(cnd)
  t = jnp.dot(sc, a_ref[...], preferred_element_type=F32, precision=lax.Precision.HIGHEST)
  o_ref[...] = jnp.dot(t, b_ref[...], preferred_element_type=F32,
                       precision=lax.Precision.HIGHEST) + bias_ref[...]


def ada_table(cond8, ada_a, ada_b, ada_bias):
  tn = 4096
  rank = ada_a.shape[-1]
  nmd = N_MOD * D_MODEL
  return pl.pallas_call(
      _ada_kernel,
      grid=(DEPTH, nmd // tn),
      in_specs=[
          pl.BlockSpec((8, D_MODEL), lambda l, j: (0, 0)),
          pl.BlockSpec((None, D_MODEL, rank), lambda l, j: (l, 0, 0)),
          pl.BlockSpec((None, rank, tn), lambda l, j: (l, 0, j)),
          pl.BlockSpec((None, 1, tn), lambda l, j: (l, 0, j)),
      ],
      out_specs=pl.BlockSpec((None, 8, tn), lambda l, j: (l, 0, j)),
      out_shape=jax.ShapeDtypeStruct((DEPTH, 8, nmd), F32),
      compiler_params=_cparams(("arbitrary", "arbitrary")),
      name="ada_table",
  )(cond8, ada_a, ada_b, ada_bias.reshape(DEPTH, 1, nmd))


def _norm_mod(x, g, shift, scale):
  ms = jnp.mean(x * x, axis=-1, keepdims=True)
  y = x * lax.rsqrt(ms + EPS) * g
  return y * (1.0 + scale) + shift


def _norm_kernel(x_ref, g_ref, shift_ref, scale_ref, o_ref):
  o_ref[...] = _norm_mod(x_ref[...], g_ref[...], shift_ref[...], scale_ref[...]).astype(BF16)


def _norm_router_kernel(x_ref, g_ref, shift_ref, scale_ref, whi_ref, wlo_ref, rb_ref,
                        o_ref, gates_ref):
  f = _norm_mod(x_ref[...], g_ref[...], shift_ref[...], scale_ref[...])
  f_hi = f.astype(BF16)
  o_ref[...] = f_hi
  f_lo = (f - f_hi.astype(F32)).astype(BF16)
  w_hi = whi_ref[...]
  logits = (jnp.dot(f_hi, w_hi, preferred_element_type=F32)
            + jnp.dot(f_lo, w_hi, preferred_element_type=F32)
            + jnp.dot(f_hi, wlo_ref[...], preferred_element_type=F32))
  scores = jax.nn.sigmoid(logits)
  lane = lax.broadcasted_iota(jnp.int32, scores.shape, 1).astype(F32)
  work = jnp.where(lane < N_EXPERTS, scores + rb_ref[...], NEG)
  sel = jnp.zeros(scores.shape, F32)
  for _ in range(TOP_K):
    m = jnp.max(work, axis=-1, keepdims=True)
    first = jnp.min(jnp.where(work == m, lane, float(LANES)), axis=-1, keepdims=True)
    hit = lane == first
    sel = jnp.where(hit, scores, sel)
    work = jnp.where(hit, 3.0 * NEG, work)
  gate = sel / jnp.sum(sel, axis=-1, keepdims=True) * ROUTED_SCALE
  shared_on = jnp.logical_and(lane >= N_EXPERTS, lane < N_GATE_COLS)
  gates_ref[...] = jnp.where(shared_on, 1.0, gate)


def norm_mod(s, gain, shift8, scale8):
  n = T_PAD // TN_NORM
  row = _mod_row(TN_NORM)
  return pl.pallas_call(
      _norm_kernel,
      grid=(n,),
      in_specs=[
          pl.BlockSpec((TN_NORM, D_MODEL), lambda i: (i, 0)),
          pl.BlockSpec((1, D_MODEL), lambda i: (0, 0)),
          pl.BlockSpec((None, 1, D_MODEL), lambda i: (row(i), 0, 0)),
          pl.BlockSpec((None, 1, D_MODEL), lambda i: (row(i), 0, 0)),
      ],
      out_specs=pl.BlockSpec((TN_NORM, D_MODEL), lambda i: (i, 0)),
      out_shape=jax.ShapeDtypeStruct((T_PAD, D_MODEL), BF16),
      compiler_params=_cparams(("parallel",)),
      name="norm_mod",
  )(s, gain.reshape(1, D_MODEL), shift8, scale8)


def norm_mod_router(s, gain, shift8, scale8, w_router, r_bias):
  n = T_PAD // TN_NORM
  row = _mod_row(TN_NORM)
  w_pad = jnp.pad(w_router, ((0, 0), (0, LANES - N_EXPERTS)))
  w_hi = w_pad.astype(BF16)
  w_lo = (w_pad - w_hi.astype(F32)).astype(BF16)
  rb = jnp.pad(r_bias.astype(F32), (0, LANES - N_EXPERTS)).reshape(1, LANES)
  return pl.pallas_call(
      _norm_router_kernel,
      grid=(n,),
      in_specs=[
          pl.BlockSpec((TN_NORM, D_MODEL), lambda i: (i, 0)),
          pl.BlockSpec((1, D_MODEL), lambda i: (0, 0)),
          pl.BlockSpec((None, 1, D_MODEL), lambda i: (row(i), 0, 0)),
          pl.BlockSpec((None, 1, D_MODEL), lambda i: (row(i), 0, 0)),
          pl.BlockSpec((D_MODEL, LANES), lambda i: (0, 0)),
          pl.BlockSpec((D_MODEL, LANES), lambda i: (0, 0)),
          pl.BlockSpec((1, LANES), lambda i: (0, 0)),
      ],
      out_specs=[
          pl.BlockSpec((TN_NORM, D_MODEL), lambda i: (i, 0)),
          pl.BlockSpec((TN_NORM, LANES), lambda i: (i, 0)),
      ],
      out_shape=[
          jax.ShapeDtypeStruct((T_PAD, D_MODEL), BF16),
          jax.ShapeDtypeStruct((T_PAD, LANES), F32),
      ],
      compiler_params=_cparams(("parallel",)),
      name="norm_mod_router",
  )(s, gain.reshape(1, D_MODEL), shift8, scale8, w_hi, w_lo, rb)


def _gelu_exact(x):
  return 0.5 * x * (1.0 + lax.erf(x * (1.0 / math.sqrt(2.0))))


def _mm_plain_kernel(x_ref, w_ref, o_ref):
  o_ref[...] = jnp.dot(x_ref[...], w_ref[...], preferred_element_type=F32).astype(o_ref.dtype)


def _mm_headnorm_kernel(x_ref, w_ref, g_ref, o_ref, *, n_norm_tiles):
  acc = jnp.dot(x_ref[...], w_ref[...], preferred_element_type=F32)
  j = pl.program_id(0)

  @pl.when(j < n_norm_tiles)
  def _():
    for h in range(acc.shape[1] // HEAD_DIM):
      blk = acc[:, h * HEAD_DIM:(h + 1) * HEAD_DIM]
      ms = jnp.mean(blk * blk, axis=-1, keepdims=True)
      y = blk * lax.rsqrt(ms + EPS) * g_ref[:, h * HEAD_DIM:(h + 1) * HEAD_DIM]
      o_ref[:, h * HEAD_DIM:(h + 1) * HEAD_DIM] = y.astype(o_ref.dtype)

  @pl.when(j >= n_norm_tiles)
  def _():
    o_ref[...] = acc.astype(o_ref.dtype)


def _mm_gelu_kernel(x_ref, w_ref, b_ref, o_ref):
  acc = jnp.dot(x_ref[...], w_ref[...], preferred_element_type=F32) + b_ref[...]
  o_ref[...] = _gelu_exact(acc).astype(o_ref.dtype)


def _mm_resid_kernel(x_ref, w_ref, s_ref, m_ref, o_ref):
  acc = jnp.dot(x_ref[...], w_ref[...], preferred_element_type=F32)
  o_ref[...] = s_ref[...] + m_ref[...] * acc


def _mm_resid_k_kernel(x_ref, w_ref, s_ref, m_ref, o_ref, acc_ref):
  k = pl.program_id(2)

  @pl.when(k == 0)
  def _():
    acc_ref[...] = jnp.zeros_like(acc_ref)

  acc_ref[...] += jnp.dot(x_ref[...], w_ref[...], preferred_element_type=F32)

  @pl.when(k == pl.num_programs(2) - 1)
  def _():
    o_ref[...] = s_ref[...] + m_ref[...] * acc_ref[...]


def _moe_up_kernel(x_ref, wg_ref, wu_ref, gate_ref, o_ref):
  x = x_ref[...]
  g = jnp.dot(x, wg_ref[...], preferred_element_type=F32)
  u = jnp.dot(x, wu_ref[...], preferred_element_type=F32)
  hid = g * jax.nn.sigmoid(g) * u
  gate = gate_ref[...]
  for e in range(EXP_PER_TILE):
    sl = slice(e * EXPERT_DIM, (e + 1) * EXPERT_DIM)
    o_ref[:, sl] = (hid[:, sl] * gate[:, e:e + 1]).astype(o_ref.dtype)


def _mm_specs(k, tn):
  return [pl.BlockSpec((TM, k), lambda j, i: (i, 0)),
          pl.BlockSpec((k, tn), lambda j, i: (0, j))]


def mm_plain(x, w, out_dtype, tn=1024):
  k, n = w.shape
  return pl.pallas_call(
      _mm_plain_kernel,
      grid=(n // tn, T_PAD // TM),
      in_specs=_mm_specs(k, tn),
      out_specs=pl.BlockSpec((TM, tn), lambda j, i: (i, j)),
      out_shape=jax.ShapeDtypeStruct((T_PAD, n), out_dtype),
      compiler_params=_cparams(("parallel", "parallel")),
      name="mm_plain",
  )(x, w)


def mm_headnorm(x, w, gains, n_norm_cols, tn=1024):
  k, n = w.shape
  return pl.pallas_call(
      functools.partial(_mm_headnorm_kernel, n_norm_tiles=n_norm_cols // tn),
      grid=(n // tn, T_PAD // TM),
      in_specs=_mm_specs(k, tn) + [pl.BlockSpec((1, tn), lambda j, i: (0, j))],
      out_specs=pl.BlockSpec((TM, tn), lambda j, i: (i, j)),
      out_shape=jax.ShapeDtypeStruct((T_PAD, n), BF16),
      compiler_params=_cparams(("parallel", "parallel")),
      name="mm_headnorm",
  )(x, w, gains)


def mm_gelu(x, w, b, tn=1024):
  k, n = w.shape
  return pl.pallas_call(
      _mm_gelu_kernel,
      grid=(n // tn, T_PAD // TM),
      in_specs=_mm_specs(k, tn) + [pl.BlockSpec((1, tn), lambda j, i: (0, j))],
      out_specs=pl.BlockSpec((TM, tn), lambda j, i: (i, j)),
      out_shape=jax.ShapeDtypeStruct((T_PAD, n), BF16),
      compiler_params=_cparams(("parallel", "parallel")),
      name="mm_gelu",
  )(x, w, b.reshape(1, n))


def mm_resid(x, w, s, gate8, tn=1024):
  k, n = w.shape
  row = _mod_row(TM)
  return pl.pallas_call(
      _mm_resid_kernel,
      grid=(n // tn, T_PAD // TM),
      in_specs=_mm_specs(k, tn) + [
          pl.BlockSpec((TM, tn), lambda j, i: (i, j)),
          pl.BlockSpec((None, 1, tn), lambda j, i: (row(i), 0, j)),
      ],
      out_specs=pl.BlockSpec((TM, tn), lambda j, i: (i, j)),
      out_shape=jax.ShapeDtypeStruct((T_PAD, n), F32),
      input_output_aliases={2: 0},
      compiler_params=_cparams(("parallel", "parallel")),
      name="mm_resid",
  )(x, w, s, gate8)


def mm_resid_k(x, w, s, gate8, tn=1024, tk=MOE_TK):
  k, n = w.shape
  row = _mod_row(TM)
  return pl.pallas_call(
      _mm_resid_k_kernel,
      grid=(n // tn, T_PAD // TM, k // tk),
      in_specs=[
          pl.BlockSpec((TM, tk), lambda j, i, kk: (i, kk)),
          pl.BlockSpec((tk, tn), lambda j, i, kk: (kk, j)),
          pl.BlockSpec((TM, tn), lambda j, i, kk: (i, j)),
          pl.BlockSpec((None, 1, tn), lambda j, i, kk: (row(i), 0, j)),
      ],
      out_specs=pl.BlockSpec((TM, tn), lambda j, i, kk: (i, j)),
      out_shape=jax.ShapeDtypeStruct((T_PAD, n), F32),
      scratch_shapes=[pltpu.VMEM((TM, tn), F32)],
      input_output_aliases={2: 0},
      compiler_params=_cparams(("parallel", "parallel", "arbitrary")),
      name="mm_resid_k",
  )(x, w, s, gate8)


def moe_up(x, wg, wu, gates_t):
  k, n = wg.shape
  tn = MOE_TN
  return pl.pallas_call(
      _moe_up_kernel,
      grid=(n // tn, T_PAD // TM),
      in_specs=_mm_specs(k, tn) + [
          pl.BlockSpec((k, tn), lambda j, i: (0, j)),
          pl.BlockSpec((None, TM, EXP_PER_TILE), lambda j, i: (j, i, 0)),
      ],
      out_specs=pl.BlockSpec((TM, tn), lambda j, i: (i, j)),
      out_shape=jax.ShapeDtypeStruct((T_PAD, n), BF16),
      compiler_params=_cparams(("parallel", "parallel")),
      name="moe_up",
  )(x, wg, wu, gates_t)


def _na_kernel(q_ref, k_ref, v_ref, kc_ref, vc_ref, bias_ref, o_ref):
  qb = pl.program_id(2)
  kb = jnp.clip(qb * NA_R - WIN_H // 2, 0, ROWS - NA_KROWS)
  start = pl.multiple_of(kb * GRID_W, GRID_W)
  kind = jnp.where(qb == 0, 0, jnp.where(qb == NA_QBLOCKS - 1, 2, 1))
  scale = HEAD_DIM ** -0.5
  nt = (((1,), (1,)), ((), ()))
  q = q_ref[...]
  kw = k_ref[pl.ds(start, NA_KTOK), :]
  vw = v_ref[pl.ds(start, NA_KTOK), :]
  s = lax.dot_general(q, kw, nt, preferred_element_type=F32) * scale + bias_ref[kind]
  sc = lax.dot_general(q, kc_ref[...], nt, preferred_element_type=F32) * scale
  m = jnp.maximum(jnp.max(s, axis=-1, keepdims=True), jnp.max(sc, axis=-1, keepdims=True))
  p = jnp.exp(s - m)
  pc = jnp.exp(sc - m)
  den = jnp.sum(p, axis=-1, keepdims=True) + jnp.sum(pc, axis=-1, keepdims=True)
  o = (jnp.dot(p.astype(BF16), vw, preferred_element_type=F32)
       + jnp.dot(pc.astype(BF16), vc_ref[...], preferred_element_type=F32))
  o_ref[...] = (o / den).astype(o_ref.dtype)


def _na_bias_index():
  ridx, cidx, valid = [], [], []
  for qb in (0, 1, NA_QBLOCKS - 1):
    r0 = qb * NA_R
    kb = int(np.clip(r0 - WIN_H // 2, 0, ROWS - NA_KROWS))
    r = r0 + np.arange(NA_Q) // GRID_W
    c = np.arange(NA_Q) % GRID_W
    rk = kb + np.arange(NA_KTOK) // GRID_W
    ck = np.arange(NA_KTOK) % GRID_W
    rs = np.clip(r - WIN_H // 2, 0, ROWS - WIN_H)
    cs = np.clip(c - WIN_W // 2, 0, GRID_W - WIN_W)
    ok = ((rk[None, :] >= rs[:, None]) & (rk[None, :] < rs[:, None] + WIN_H)
          & (ck[None, :] >= cs[:, None]) & (ck[None, :] < cs[:, None] + WIN_W))
    ridx.append(np.clip(rk[None, :] - r[:, None] + WIN_H - 1, 0, 2 * WIN_H - 2))
    cidx.append(np.clip(ck[None, :] - c[:, None] + WIN_W - 1, 0, 2 * WIN_W - 2))
    valid.append(ok)
  return np.stack(ridx), np.stack(cidx), np.stack(valid)


def na_bias(rpb):
  ridx, cidx, valid = _na_bias_index()
  g = rpb.astype(F32)[:, ridx, cidx]
  return jnp.where(valid[None], g, NEG).transpose(1, 0, 2, 3)


def na_attention(qkv, bias):
  hq = NA_HEADS
  lat_blocks = SEQ // NA_Q
  ctx_blk0 = CTX_ROW0 // CTX_LEN
  return pl.pallas_call(
      _na_kernel,
      grid=(BATCH, NA_HEADS, NA_QBLOCKS),
      in_specs=[
          pl.BlockSpec((NA_Q, HEAD_DIM), lambda b, h, i: (b * lat_blocks + i, h)),
          pl.BlockSpec((SEQ, HEAD_DIM), lambda b, h, i: (b, hq + h)),
          pl.BlockSpec((SEQ, HEAD_DIM), lambda b, h, i: (b, 2 * hq + h)),
          pl.BlockSpec((CTX_LEN, HEAD_DIM), lambda b, h, i: (ctx_blk0 + b, hq + h)),
          pl.BlockSpec((CTX_LEN, HEAD_DIM), lambda b, h, i: (ctx_blk0 + b, 2 * hq + h)),
          pl.BlockSpec((3, None, NA_Q, NA_KTOK), lambda b, h, i: (0, h, 0, 0)),
      ],
      out_specs=pl.BlockSpec((NA_Q, HEAD_DIM), lambda b, h, i: (b * lat_blocks + i, h)),
      out_shape=jax.ShapeDtypeStruct((N_LAT, NA_WIDTH), BF16),
      compiler_params=_cparams(("parallel", "parallel", "arbitrary")),
      name="na_attention",
  )(qkv, qkv, qkv, qkv, qkv, bias)


def _ctx_attn_kernel(q_ref, k_ref, v_ref, o_ref):
  scale = HEAD_DIM ** -0.5
  s = lax.dot_general(q_ref[...], k_ref[...], (((1,), (1,)), ((), ())),
                      preferred_element_type=F32) * scale
  m = jnp.max(s, axis=-1, keepdims=True)
  p = jnp.exp(s - m)
  den = jnp.sum(p, axis=-1, keepdims=True)
  o = jnp.dot(p.astype(BF16), v_ref[...], preferred_element_type=F32)
  o_ref[...] = (o / den).astype(o_ref.dtype)


def ctx_attention(qkv):
  hq = NA_HEADS
  ctx_blk0 = CTX_ROW0 // CTX_LEN
  return pl.pallas_call(
      _ctx_attn_kernel,
      grid=(BATCH, NA_HEADS),
      in_specs=[
          pl.BlockSpec((CTX_LEN, HEAD_DIM), lambda b, h: (ctx_blk0 + b, h)),
          pl.BlockSpec((CTX_LEN, HEAD_DIM), lambda b, h: (ctx_blk0 + b, hq + h)),
          pl.BlockSpec((CTX_LEN, HEAD_DIM), lambda b, h: (ctx_blk0 + b, 2 * hq + h)),
      ],
      out_specs=pl.BlockSpec((CTX_LEN, HEAD_DIM), lambda b, h: (b, h)),
      out_shape=jax.ShapeDtypeStruct((N_CTX, NA_WIDTH), BF16),
      compiler_params=_cparams(("parallel", "parallel")),
      name="ctx_attention",
  )(qkv, qkv, qkv)


def _gmlp_gate_kernel(u_ref, v_ref, vg_ref, ws_ref, bs_ref, o_ref):
  v = v_ref[...].astype(F32)
  ms = jnp.mean(v * v, axis=-1, keepdims=True)
  vn = (v * lax.rsqrt(ms + EPS) * vg_ref[...]).astype(BF16)
  for c in range(v.shape[0] // CHUNK):
    rows = slice(c * CHUNK, (c + 1) * CHUNK)
    for g in range(GM_GROUPS):
      cols = slice(g * HEAD_DIM, (g + 1) * HEAD_DIM)
      sp = jnp.dot(ws_ref[g], vn[rows, cols], preferred_element_type=F32) + bs_ref[:, cols]
      o_ref[rows, cols] = (u_ref[rows, cols].astype(F32) * sp).astype(o_ref.dtype)


def gmlp_gate(z, v_gain, w_s, bsx):
  tg = 256
  ncol = GM_WIDTH // GM_WIDTH
  return pl.pallas_call(
      _gmlp_gate_kernel,
      grid=(T_PAD // tg,),
      in_specs=[
          pl.BlockSpec((tg, GM_WIDTH), lambda i: (i, 0)),
          pl.BlockSpec((tg, GM_WIDTH), lambda i: (i, ncol)),
          pl.BlockSpec((1, GM_WIDTH), lambda i: (0, 0)),
          pl.BlockSpec((GM_GROUPS, CHUNK, CHUNK), lambda i: (0, 0, 0)),
          pl.BlockSpec((CHUNK, GM_WIDTH), lambda i: (0, 0)),
      ],
      out_specs=pl.BlockSpec((tg, GM_WIDTH), lambda i: (i, 0)),
      out_shape=jax.ShapeDtypeStruct((T_PAD, GM_WIDTH), BF16),
      compiler_params=_cparams(("parallel",)),
      name="gmlp_gate",
  )(z, z, v_gain.reshape(1, GM_WIDTH), w_s, bsx)


def _short_conv(x, w, b):
  n = x.shape[1]
  pad = HY_SHORT // 2
  xp = jnp.pad(x, ((0, 0), (pad, HY_SHORT - 1 - pad), (0, 0)))
  out = b
  for i in range(HY_SHORT):
    out = out + xp[:, i:i + n] * w[i]
  return out


def _hyena_filters(length, w1, b1, w2, b2, w3, b3, w4, freq):
  hp = lax.Precision.HIGHEST
  t_norm = jnp.linspace(0.0, 1.0, length, dtype=F32)[:, None]
  pos = jnp.arange(length, dtype=F32)[:, None]
  bands = jnp.linspace(1e-4, HY_BANDS - 1, HY_BANDS, dtype=F32)[None, :]
  ang = (2.0 * math.pi / length) * pos * bands
  feats = jnp.concatenate([t_norm, jnp.cos(ang), -jnp.sin(ang)], axis=-1)
  hdn = jnp.sin(freq[0] * (jnp.dot(feats, w1, precision=hp) + b1))
  hdn = jnp.sin(freq[1] * (jnp.dot(hdn, w2, precision=hp) + b2))
  hdn = jnp.sin(freq[2] * (jnp.dot(hdn, w3, precision=hp) + b3))
  filt = jnp.dot(hdn, w4, precision=hp).reshape(length, 2, HY_ORDER, HY_WIDTH)
  deltas = jnp.abs(jnp.linspace(math.log(HY_DECAY_TARGET) / HY_FAST_DECAY,
                                math.log(HY_DECAY_TARGET) / HY_SLOW_DECAY, HY_WIDTH, dtype=F32))
  filt = filt * jnp.exp(-t_norm * deltas)[:, None, None, :]
  fwd, bwd = filt[:, 0], filt[:, 1]
  kern = jnp.concatenate([fwd, jnp.zeros_like(fwd[:1]), bwd[:0:-1]], axis=0)
  return kern * lax.rsqrt(jnp.sum(kern * kern, axis=0, keepdims=True) + EPS)


def _long_conv(z, kern, skip):
  n = z.shape[1]
  zk = jnp.fft.rfft(z, n=2 * n, axis=1) * jnp.fft.rfft(kern, axis=0)[None]
  y = jnp.fft.irfft(zk, n=2 * n, axis=1)[:, :n]
  return y + z * skip


def _hyena(p, conv_w, conv_b, w1, b1, w2, b2, w3, b3, w4, freq, skip):
  u = _short_conv(p, conv_w, conv_b)
  parts = jnp.split(u, HY_ORDER + 1, axis=-1)
  gates, z = parts[:HY_ORDER], parts[HY_ORDER]
  kern = _hyena_filters(p.shape[1], w1, b1, w2, b2, w3, b3, w4, freq)
  for o in range(HY_ORDER):
    z = gates[o] * _long_conv(z, kern[:, o], skip[o])
  return z


def kernel(x, c, ctx, c_ctx, ada_a, ada_b, ada_bias, norm_mix, norm_ffn, even_w_in, na_q_gain,
           na_k_gain, na_rpb, hy_conv_w, hy_conv_b, hy_w1, hy_b1, hy_w2, hy_b2, hy_w3, hy_b3,
           hy_w4, hy_freq, hy_skip, odd_w_in, odd_b_in, gm_v_gain, gm_w_s, gm_b_s, w_mix_out,
           moe_router, moe_bias, moe_w_gate, moe_w_up, moe_w_down, sh_w_gate, sh_w_up, sh_w_down):
  s = jnp.concatenate([x.reshape(N_LAT, D_MODEL), ctx.reshape(N_CTX, D_MODEL),
                       jnp.zeros((T_PAD - T_REAL, D_MODEL), F32)], axis=0)
  cond8 = jnp.concatenate([c, c_ctx[None], jnp.zeros((8 - BATCH - 1, D_MODEL), F32)], axis=0)
  mods = ada_table(cond8, ada_a, ada_b, ada_bias).reshape(DEPTH, 8, N_MOD, 1, D_MODEL)

  for layer in range(DEPTH):
    j = layer // 2
    mod = [mods[layer, :, i] for i in range(N_MOD)]
    h = norm_mod(s, norm_mix[layer], mod[0], mod[1])
    w_out = w_mix_out[layer].astype(BF16)
    if layer % 2 == 0:
      w_in = even_w_in[j].astype(BF16)
      gains = jnp.concatenate([jnp.tile(na_q_gain[j], NA_HEADS), jnp.tile(na_k_gain[j], NA_HEADS),
                               jnp.ones((NA_WIDTH,), F32)]).reshape(1, 3 * NA_WIDTH)
      qkv = mm_headnorm(h, w_in[:, :3 * NA_WIDTH], gains, 2 * NA_WIDTH)
      hy = mm_plain(h, w_in[:, 3 * NA_WIDTH:], F32)
      a_lat = na_attention(qkv, na_bias(na_rpb[j]))
      a_ctx = ctx_attention(qkv)
      hy_params = (hy_conv_w[j], hy_conv_b[j], hy_w1[j], hy_b1[j], hy_w2[j], hy_b2[j],
                   hy_w3[j], hy_b3[j], hy_w4[j], hy_freq[j], hy_skip[j])
      z_lat = _hyena(hy[:N_LAT].reshape(BATCH, SEQ, -1), *hy_params).reshape(N_LAT, HY_WIDTH)
      z_ctx = _hyena(hy[CTX_ROW0:T_REAL].reshape(BATCH, CTX_LEN, -1), *hy_params).reshape(N_CTX, HY_WIDTH)
      mix = jnp.concatenate([
          jnp.concatenate([a_lat, z_lat.astype(BF16)], axis=1),
          jnp.concatenate([a_ctx, z_ctx.astype(BF16)], axis=1),
          jnp.zeros((T_PAD - T_REAL, D_MODEL), BF16)], axis=0)
    else:
      z = mm_gelu(h, odd_w_in[j].astype(BF16), odd_b_in[j])
      bsx = jnp.repeat(gm_b_s[j].T, HEAD_DIM, axis=1)
      mix = gmlp_gate(z, gm_v_gain[j], gm_w_s[j].astype(BF16), bsx)
    s = mm_resid(mix, w_out, s, mod[2])

    f, gates = norm_mod_router(s, norm_ffn[layer], mod[3], mod[4], moe_router[layer], moe_bias[layer])
    gates_t = gates[:, :N_GATE_COLS].reshape(T_PAD, N_GATE_COLS // EXP_PER_TILE, EXP_PER_TILE)
    gates_t = gates_t.transpose(1, 0, 2)
    wg = jnp.concatenate([moe_w_gate[layer].transpose(1, 0, 2).reshape(D_MODEL, -1),
                          sh_w_gate[layer]], axis=1).astype(BF16)
    wu = jnp.concatenate([moe_w_up[layer].transpose(1, 0, 2).reshape(D_MODEL, -1),
                          sh_w_up[layer]], axis=1).astype(BF16)
    wd = jnp.concatenate([moe_w_down[layer].reshape(-1, D_MODEL), sh_w_down[layer]],
                         axis=0).astype(BF16)
    hid = moe_up(f, wg, wu, gates_t)
    s = mm_resid_k(hid, wd, s, mod[5])

  return s[:N_LAT].reshape(BATCH, SEQ, D_MODEL)
```

```python
import functools
import math

import numpy as np
import jax
import jax.numpy as jnp
from jax import lax
from jax.experimental import pallas as pl
from jax.experimental.pallas import tpu as pltpu

F32 = jnp.float32
BF16 = jnp.bfloat16

D_MODEL = 4096
BATCH = 2
SEQ = 8192
DEPTH = 4
GRID_W = 64
ROWS = SEQ // GRID_W
CTX_LEN = 256
HEAD_DIM = 128
NA_HEADS = 16
NA_WIDTH = NA_HEADS * HEAD_DIM
HY_WIDTH = D_MODEL - NA_WIDTH
WIN_H = 8
WIN_W = 16
HY_ORDER = 2
HY_SHORT = 3
HY_BANDS = 16
HY_DECAY_TARGET = 1e-2
HY_FAST_DECAY = 0.3
HY_SLOW_DECAY = 1.5
GM_WIDTH = D_MODEL
GM_GROUPS = GM_WIDTH // HEAD_DIM
CHUNK = 128
N_EXPERTS = 64
TOP_K = 8
EXPERT_DIM = 128
SHARED_DIM = TOP_K * EXPERT_DIM
ROUTED_SCALE = 2.5
N_MOD = 6
EPS = 1e-6

LANES = 128
VMEM_LIMIT = 56 * 1024 * 1024

TM = 1024
N_LAT = BATCH * SEQ
N_CTX = BATCH * CTX_LEN
T_REAL = N_LAT + N_CTX
T_PAD = -(-T_REAL // TM) * TM
CTX_ROW0 = N_LAT
TN_NORM = 512
MOE_HID = N_EXPERTS * EXPERT_DIM + SHARED_DIM
MOE_TN = 512
EXP_PER_TILE = MOE_TN // EXPERT_DIM
N_GATE_COLS = MOE_HID // EXPERT_DIM
MOE_TK = 2304
NEG = -1e30

NA_R = 4
NA_Q = NA_R * GRID_W
NA_KROWS = NA_R + WIN_H
NA_KTOK = NA_KROWS * GRID_W
NA_QBLOCKS = ROWS // NA_R


def _mod_row(tile_rows):
  def f(i):
    return jnp.minimum((i * tile_rows) // SEQ, BATCH)
  return f


def _cparams(sem):
  return pltpu.CompilerParams(dimension_semantics=sem, vmem_limit_bytes=VMEM_LIMIT)


def _ada_kernel(cond_ref, a_ref, b_ref, bias_ref, o_ref):
  cnd = cond_ref[...]
  sc = cnd * jax.nn.sigmoid(cnd)
  t = jnp.dot(sc, a_ref[...], preferred_element_type=F32, precision=lax.Precision.HIGHEST)
  o_ref[...] = jnp.dot(t, b_ref[...], preferred_element_type=F32,
                       precision=lax.Precision.HIGHEST) + bias_ref[...]


def ada_table(cond8, ada_a, ada_b, ada_bias):
  tn = 4096
  rank = ada_a.shape[-1]
  nmd = N_MOD * D_MODEL
  return pl.pallas_call(
      _ada_kernel,
      grid=(DEPTH, nmd // tn),
      in_specs=[
          pl.BlockSpec((8, D_MODEL), lambda l, j: (0, 0)),
          pl.BlockSpec((None, D_MODEL, rank), lambda l, j: (l, 0, 0)),
          pl.BlockSpec((None, rank, tn), lambda l, j: (l, 0, j)),
          pl.BlockSpec((None, 1, tn), lambda l, j: (l, 0, j)),
      ],
      out_specs=pl.BlockSpec((None, 8, tn), lambda l, j: (l, 0, j)),
      out_shape=jax.ShapeDtypeStruct((DEPTH, 8, nmd), F32),
      compiler_params=_cparams(("arbitrary", "arbitrary")),
      name="ada_table",
  )(cond8, ada_a, ada_b, ada_bias.reshape(DEPTH, 1, nmd))


def _norm_mod(x, g, shift, scale):
  ms = jnp.mean(x * x, axis=-1, keepdims=True)
  y = x * lax.rsqrt(ms + EPS) * g
  return y * (1.0 + scale) + shift


def _norm_kernel(x_ref, g_ref, shift_ref, scale_ref, o_ref):
  o_ref[...] = _norm_mod(x_ref[...], g_ref[...], shift_ref[...], scale_ref[...]).astype(BF16)


def _norm_router_kernel(x_ref, g_ref, shift_ref, scale_ref, whi_ref, wlo_ref, rb_ref,
                        o_ref, gates_ref):
  f = _norm_mod(x_ref[...], g_ref[...], shift_ref[...], scale_ref[...])
  f_hi = f.astype(BF16)
  o_ref[...] = f_hi
  f_lo = (f - f_hi.astype(F32)).astype(BF16)
  w_hi = whi_ref[...]
  logits = (jnp.dot(f_hi, w_hi, preferred_element_type=F32)
            + jnp.dot(f_lo, w_hi, preferred_element_type=F32)
            + jnp.dot(f_hi, wlo_ref[...], preferred_element_type=F32))
  scores = jax.nn.sigmoid(logits)
  lane = lax.broadcasted_iota(jnp.int32, scores.shape, 1).astype(F32)
  work = jnp.where(lane < N_EXPERTS, scores + rb_ref[...], NEG)
  sel = jnp.zeros(scores.shape, F32)
  for _ in range(TOP_K):
    m = jnp.max(work, axis=-1, keepdims=True)
    first = jnp.min(jnp.where(work == m, lane, float(LANES)), axis=-1, keepdims=True)
    hit = lane == first
    sel = jnp.where(hit, scores, sel)
    work = jnp.where(hit, 3.0 * NEG, work)
  gate = sel / jnp.sum(sel, axis=-1, keepdims=True) * ROUTED_SCALE
  shared_on = jnp.logical_and(lane >= N_EXPERTS, lane < N_GATE_COLS)
  gates_ref[...] = jnp.where(shared_on, 1.0, gate)


def norm_mod(s, gain, shift8, scale8):
  n = T_PAD // TN_NORM
  row = _mod_row(TN_NORM)
  return pl.pallas_call(
      _norm_kernel,
      grid=(n,),
      in_specs=[
          pl.BlockSpec((TN_NORM, D_MODEL), lambda i: (i, 0)),
          pl.BlockSpec((1, D_MODEL), lambda i: (0, 0)),
          pl.BlockSpec((None, 1, D_MODEL), lambda i: (row(i), 0, 0)),
          pl.BlockSpec((None, 1, D_MODEL), lambda i: (row(i), 0, 0)),
      ],
      out_specs=pl.BlockSpec((TN_NORM, D_MODEL), lambda i: (i, 0)),
      out_shape=jax.ShapeDtypeStruct((T_PAD, D_MODEL), BF16),
      compiler_params=_cparams(("parallel",)),
      name="norm_mod",
  )(s, gain.reshape(1, D_MODEL), shift8, scale8)


def norm_mod_router(s, gain, shift8, scale8, w_router, r_bias):
  n = T_PAD // TN_NORM
  row = _mod_row(TN_NORM)
  w_pad = jnp.pad(w_router, ((0, 0), (0, LANES - N_EXPERTS)))
  w_hi = w_pad.astype(BF16)
  w_lo = (w_pad - w_hi.astype(F32)).astype(BF16)
  rb = jnp.pad(r_bias.astype(F32), (0, LANES - N_EXPERTS)).reshape(1, LANES)
  return pl.pallas_call(
      _norm_router_kernel,
      grid=(n,),
      in_specs=[
          pl.BlockSpec((TN_NORM, D_MODEL), lambda i: (i, 0)),
          pl.BlockSpec((1, D_MODEL), lambda i: (0, 0)),
          pl.BlockSpec((None, 1, D_MODEL), lambda i: (row(i), 0, 0)),
          pl.BlockSpec((None, 1, D_MODEL), lambda i: (row(i), 0, 0)),
          pl.BlockSpec((D_MODEL, LANES), lambda i: (0, 0)),
          pl.BlockSpec((D_MODEL, LANES), lambda i: (0, 0)),
          pl.BlockSpec((1, LANES), lambda i: (0, 0)),
      ],
      out_specs=[
          pl.BlockSpec((TN_NORM, D_MODEL), lambda i: (i, 0)),
          pl.BlockSpec((TN_NORM, LANES), lambda i: (i, 0)),
      ],
      out_shape=[
          jax.ShapeDtypeStruct((T_PAD, D_MODEL), BF16),
          jax.ShapeDtypeStruct((T_PAD, LANES), F32),
      ],
      compiler_params=_cparams(("parallel",)),
      name="norm_mod_router",
  )(s, gain.reshape(1, D_MODEL), shift8, scale8, w_hi, w_lo, rb)


def _gelu_exact(x):
  return 0.5 * x * (1.0 + lax.erf(x * (1.0 / math.sqrt(2.0))))


def _mm_headnorm_kernel(x_ref, w_ref, g_ref, o_ref, *, n_norm_tiles):
  acc = jnp.dot(x_ref[...], w_ref[...], preferred_element_type=F32)
  j = pl.program_id(0)

  @pl.when(j < n_norm_tiles)
  def _():
    for h in range(acc.shape[1] // HEAD_DIM):
      blk = acc[:, h * HEAD_DIM:(h + 1) * HEAD_DIM]
      ms = jnp.mean(blk * blk, axis=-1, keepdims=True)
      y = blk * lax.rsqrt(ms + EPS) * g_ref[:, h * HEAD_DIM:(h + 1) * HEAD_DIM]
      o_ref[:, h * HEAD_DIM:(h + 1) * HEAD_DIM] = y.astype(o_ref.dtype)

  @pl.when(j >= n_norm_tiles)
  def _():
    o_ref[...] = acc.astype(o_ref.dtype)


def _mm_gelu_kernel(x_ref, w_ref, b_ref, o_ref):
  acc = jnp.dot(x_ref[...], w_ref[...], preferred_element_type=F32) + b_ref[...]
  o_ref[...] = _gelu_exact(acc).astype(o_ref.dtype)


def _mm_resid_kernel(x_ref, w_ref, s_ref, m_ref, o_ref):
  acc = jnp.dot(x_ref[...], w_ref[...], preferred_element_type=F32)
  o_ref[...] = s_ref[...] + m_ref[...] * acc


def _mm_resid_k_kernel(x_ref, w_ref, s_ref, m_ref, o_ref, acc_ref):
  k = pl.program_id(2)

  @pl.when(k == 0)
  def _():
    acc_ref[...] = jnp.zeros_like(acc_ref)

  acc_ref[...] += jnp.dot(x_ref[...], w_ref[...], preferred_element_type=F32)

  @pl.when(k == pl.num_programs(2) - 1)
  def _():
    o_ref[...] = s_ref[...] + m_ref[...] * acc_ref[...]


def _moe_up_kernel(x_ref, wg_ref, wu_ref, gate_ref, o_ref):
  x = x_ref[...]
  g = jnp.dot(x, wg_ref[...], preferred_element_type=F32)
  u = jnp.dot(x, wu_ref[...], preferred_element_type=F32)
  hid = g * jax.nn.sigmoid(g) * u
  gate = gate_ref[...]
  for e in range(EXP_PER_TILE):
    sl = slice(e * EXPERT_DIM, (e + 1) * EXPERT_DIM)
    o_ref[:, sl] = (hid[:, sl] * gate[:, e:e + 1]).astype(o_ref.dtype)


def _mm_specs(k, tn):
  return [pl.BlockSpec((TM, k), lambda j, i: (i, 0)),
          pl.BlockSpec((k, tn), lambda j, i: (0, j))]


def mm_headnorm(x, w, gains, n_norm_cols, tn=1024):
  k, n = w.shape
  return pl.pallas_call(
      functools.partial(_mm_headnorm_kernel, n_norm_tiles=n_norm_cols // tn),
      grid=(n // tn, T_PAD // TM),
      in_specs=_mm_specs(k, tn) + [pl.BlockSpec((1, tn), lambda j, i: (0, j))],
      out_specs=pl.BlockSpec((TM, tn), lambda j, i: (i, j)),
      out_shape=jax.ShapeDtypeStruct((T_PAD, n), BF16),
      compiler_params=_cparams(("parallel", "parallel")),
      name="mm_headnorm",
  )(x, w, gains)


def mm_gelu(x, w, b, tn=1024):
  k, n = w.shape
  return pl.pallas_call(
      _mm_gelu_kernel,
      grid=(n // tn, T_PAD // TM),
      in_specs=_mm_specs(k, tn) + [pl.BlockSpec((1, tn), lambda j, i: (0, j))],
      out_specs=pl.BlockSpec((TM, tn), lambda j, i: (i, j)),
      out_shape=jax.ShapeDtypeStruct((T_PAD, n), BF16),
      compiler_params=_cparams(("parallel", "parallel")),
      name="mm_gelu",
  )(x, w, b.reshape(1, n))


def mm_resid(x, w, s, gate8, tn=1024):
  k, n = w.shape
  row = _mod_row(TM)
  return pl.pallas_call(
      _mm_resid_kernel,
      grid=(n // tn, T_PAD // TM),
      in_specs=_mm_specs(k, tn) + [
          pl.BlockSpec((TM, tn), lambda j, i: (i, j)),
          pl.BlockSpec((None, 1, tn), lambda j, i: (row(i), 0, j)),
      ],
      out_specs=pl.BlockSpec((TM, tn), lambda j, i: (i, j)),
      out_shape=jax.ShapeDtypeStruct((T_PAD, n), F32),
      input_output_aliases={2: 0},
      compiler_params=_cparams(("parallel", "parallel")),
      name="mm_resid",
  )(x, w, s, gate8)


def mm_resid_k(x, w, s, gate8, tn=1024, tk=MOE_TK):
  k, n = w.shape
  row = _mod_row(TM)
  return pl.pallas_call(
      _mm_resid_k_kernel,
      grid=(n // tn, T_PAD // TM, k // tk),
      in_specs=[
          pl.BlockSpec((TM, tk), lambda j, i, kk: (i, kk)),
          pl.BlockSpec((tk, tn), lambda j, i, kk: (kk, j)),
          pl.BlockSpec((TM, tn), lambda j, i, kk: (i, j)),
          pl.BlockSpec((None, 1, tn), lambda j, i, kk: (row(i), 0, j)),
      ],
      out_specs=pl.BlockSpec((TM, tn), lambda j, i, kk: (i, j)),
      out_shape=jax.ShapeDtypeStruct((T_PAD, n), F32),
      scratch_shapes=[pltpu.VMEM((TM, tn), F32)],
      input_output_aliases={2: 0},
      compiler_params=_cparams(("parallel", "parallel", "arbitrary")),
      name="mm_resid_k",
  )(x, w, s, gate8)


def moe_up(x, wg, wu, gates_t):
  k, n = wg.shape
  tn = MOE_TN
  return pl.pallas_call(
      _moe_up_kernel,
      grid=(n // tn, T_PAD // TM),
      in_specs=_mm_specs(k, tn) + [
          pl.BlockSpec((k, tn), lambda j, i: (0, j)),
          pl.BlockSpec((None, TM, EXP_PER_TILE), lambda j, i: (j, i, 0)),
      ],
      out_specs=pl.BlockSpec((TM, tn), lambda j, i: (i, j)),
      out_shape=jax.ShapeDtypeStruct((T_PAD, n), BF16),
      compiler_params=_cparams(("parallel", "parallel")),
      name="moe_up",
  )(x, wg, wu, gates_t)


def _na_kernel(q_ref, k_ref, v_ref, kc_ref, vc_ref, bias_ref, o_ref):
  qb = pl.program_id(2)
  kb = jnp.clip(qb * NA_R - WIN_H // 2, 0, ROWS - NA_KROWS)
  start = pl.multiple_of(kb * GRID_W, GRID_W)
  kind = jnp.where(qb == 0, 0, jnp.where(qb == NA_QBLOCKS - 1, 2, 1))
  scale = HEAD_DIM ** -0.5
  nt = (((1,), (1,)), ((), ()))
  q = q_ref[...]
  kw = k_ref[pl.ds(start, NA_KTOK), :]
  vw = v_ref[pl.ds(start, NA_KTOK), :]
  s = lax.dot_general(q, kw, nt, preferred_element_type=F32) * scale + bias_ref[kind]
  sc = lax.dot_general(q, kc_ref[...], nt, preferred_element_type=F32) * scale
  m = jnp.maximum(jnp.max(s, axis=-1, keepdims=True), jnp.max(sc, axis=-1, keepdims=True))
  p = jnp.exp(s - m)
  pc = jnp.exp(sc - m)
  den = jnp.sum(p, axis=-1, keepdims=True) + jnp.sum(pc, axis=-1, keepdims=True)
  o = (jnp.dot(p.astype(BF16), vw, preferred_element_type=F32)
       + jnp.dot(pc.astype(BF16), vc_ref[...], preferred_element_type=F32))
  o_ref[...] = (o / den).astype(o_ref.dtype)


def _na_bias_selectors():
  rows = np.zeros((3, NA_R, NA_KROWS, 2 * WIN_H - 1), np.float32)
  for kind, qb in enumerate((0, 1, NA_QBLOCKS - 1)):
    r0 = qb * NA_R
    kb = int(np.clip(r0 - WIN_H // 2, 0, ROWS - NA_KROWS))
    for i in range(NA_R):
      r = r0 + i
      rs = int(np.clip(r - WIN_H // 2, 0, ROWS - WIN_H))
      for j in range(NA_KROWS):
        if rs <= kb + j < rs + WIN_H:
          rows[kind, i, j, kb + j - r + WIN_H - 1] = 1.0
  cols = np.zeros((2 * WIN_W - 1, GRID_W, GRID_W), np.float32)
  for c in range(GRID_W):
    cs = int(np.clip(c - WIN_W // 2, 0, GRID_W - WIN_W))
    for d in range(cs, cs + WIN_W):
      cols[d - c + WIN_W - 1, c, d] = 1.0
  return rows, cols


def na_bias(rpb):
  rows, cols = _na_bias_selectors()
  hp = lax.Precision.HIGHEST
  colband = jnp.einsum('hrw,wcd->hrcd', rpb.astype(F32), cols, precision=hp)
  band = jnp.einsum('tijr,hrcd->thicjd', rows, colband, precision=hp)
  valid = np.einsum('tijr,wcd->ticjd', rows, cols) > 0.5
  band = jnp.where(valid[:, None], band, NEG)
  return band.reshape(3, NA_HEADS, NA_Q, NA_KTOK)


def na_attention(qkv, bias):
  hq = NA_HEADS
  lat_blocks = SEQ // NA_Q
  ctx_blk0 = CTX_ROW0 // CTX_LEN
  return pl.pallas_call(
      _na_kernel,
      grid=(BATCH, NA_HEADS, NA_QBLOCKS),
      in_specs=[
          pl.BlockSpec((NA_Q, HEAD_DIM), lambda b, h, i: (b * lat_blocks + i, h)),
          pl.BlockSpec((SEQ, HEAD_DIM), lambda b, h, i: (b, hq + h)),
          pl.BlockSpec((SEQ, HEAD_DIM), lambda b, h, i: (b, 2 * hq + h)),
          pl.BlockSpec((CTX_LEN, HEAD_DIM), lambda b, h, i: (ctx_blk0 + b, hq + h)),
          pl.BlockSpec((CTX_LEN, HEAD_DIM), lambda b, h, i: (ctx_blk0 + b, 2 * hq + h)),
          pl.BlockSpec((3, None, NA_Q, NA_KTOK), lambda b, h, i: (0, h, 0, 0)),
      ],
      out_specs=pl.BlockSpec((NA_Q, HEAD_DIM), lambda b, h, i: (b * lat_blocks + i, h)),
      out_shape=jax.ShapeDtypeStruct((N_LAT, NA_WIDTH), BF16),
      compiler_params=_cparams(("parallel", "parallel", "arbitrary")),
      name="na_attention",
  )(qkv, qkv, qkv, qkv, qkv, bias)


def _ctx_attn_kernel(q_ref, k_ref, v_ref, o_ref):
  scale = HEAD_DIM ** -0.5
  s = lax.dot_general(q_ref[...], k_ref[...], (((1,), (1,)), ((), ())),
                      preferred_element_type=F32) * scale
  m = jnp.max(s, axis=-1, keepdims=True)
  p = jnp.exp(s - m)
  den = jnp.sum(p, axis=-1, keepdims=True)
  o = jnp.dot(p.astype(BF16), v_ref[...], preferred_element_type=F32)
  o_ref[...] = (o / den).astype(o_ref.dtype)


def ctx_attention(qkv):
  hq = NA_HEADS
  ctx_blk0 = CTX_ROW0 // CTX_LEN
  return pl.pallas_call(
      _ctx_attn_kernel,
      grid=(BATCH, NA_HEADS),
      in_specs=[
          pl.BlockSpec((CTX_LEN, HEAD_DIM), lambda b, h: (ctx_blk0 + b, h)),
          pl.BlockSpec((CTX_LEN, HEAD_DIM), lambda b, h: (ctx_blk0 + b, hq + h)),
          pl.BlockSpec((CTX_LEN, HEAD_DIM), lambda b, h: (ctx_blk0 + b, 2 * hq + h)),
      ],
      out_specs=pl.BlockSpec((CTX_LEN, HEAD_DIM), lambda b, h: (b, h)),
      out_shape=jax.ShapeDtypeStruct((N_CTX, NA_WIDTH), BF16),
      compiler_params=_cparams(("parallel", "parallel")),
      name="ctx_attention",
  )(qkv, qkv, qkv)


def _gmlp_gate_kernel(u_ref, v_ref, vg_ref, ws_ref, bs_ref, o_ref):
  v = v_ref[...].astype(F32)
  ms = jnp.mean(v * v, axis=-1, keepdims=True)
  vn = (v * lax.rsqrt(ms + EPS) * vg_ref[...]).astype(BF16)
  for c in range(v.shape[0] // CHUNK):
    rows = slice(c * CHUNK, (c + 1) * CHUNK)
    for g in range(GM_GROUPS):
      cols = slice(g * HEAD_DIM, (g + 1) * HEAD_DIM)
      sp = jnp.dot(ws_ref[g], vn[rows, cols], preferred_element_type=F32) + bs_ref[:, cols]
      o_ref[rows, cols] = (u_ref[rows, cols].astype(F32) * sp).astype(o_ref.dtype)


def gmlp_gate(z, v_gain, w_s, bsx):
  tg = 256
  ncol = GM_WIDTH // GM_WIDTH
  return pl.pallas_call(
      _gmlp_gate_kernel,
      grid=(T_PAD // tg,),
      in_specs=[
          pl.BlockSpec((tg, GM_WIDTH), lambda i: (i, 0)),
          pl.BlockSpec((tg, GM_WIDTH), lambda i: (i, ncol)),
          pl.BlockSpec((1, GM_WIDTH), lambda i: (0, 0)),
          pl.BlockSpec((GM_GROUPS, CHUNK, CHUNK), lambda i: (0, 0, 0)),
          pl.BlockSpec((CHUNK, GM_WIDTH), lambda i: (0, 0)),
      ],
      out_specs=pl.BlockSpec((tg, GM_WIDTH), lambda i: (i, 0)),
      out_shape=jax.ShapeDtypeStruct((T_PAD, GM_WIDTH), BF16),
      compiler_params=_cparams(("parallel",)),
      name="gmlp_gate",
  )(z, z, v_gain.reshape(1, GM_WIDTH), w_s, bsx)


HY_FEAT = 1 + 2 * HY_BANDS
HY_FEAT_PAD = 40
HY_HID = 64
DFT_R = 128
HY_N_LAT = 2 * SEQ
HY_N_CTX = 2 * CTX_LEN
HY_CT = 32
HY_CG = 8
assert HY_N_LAT == DFT_R * DFT_R


def _dft_consts_lat():
  n = np.arange(DFT_R)
  ang = 2.0 * np.pi * np.outer(n, n) / DFT_R
  c, s = np.cos(ang), np.sin(ang)
  tw = 2.0 * np.pi * np.outer(n, n) / HY_N_LAT
  half = SEQ // DFT_R
  fa = np.concatenate([c, -s], axis=0)
  gc = np.block([[c, -s], [s, c]])
  gci = np.block([[c, s], [-s, c]])
  fai = np.concatenate([c[:half], -s[:half]], axis=1) / HY_N_LAT
  return dict(
      fa_full=jnp.asarray(fa, BF16), fa_half=jnp.asarray(fa[:, :half], BF16),
      tw=jnp.asarray(np.stack([np.cos(tw), np.sin(tw)]), F32),
      gc=jnp.asarray(gc, BF16), gci=jnp.asarray(gci, BF16), fai=jnp.asarray(fai, BF16))


def _dft_consts_ctx():
  n = np.arange(HY_N_CTX)
  ang = 2.0 * np.pi * np.outer(n, n) / HY_N_CTX
  c, s = np.cos(ang), np.sin(ang)
  fwd = np.concatenate([c, -s], axis=1)
  inv = np.concatenate([c[:, :CTX_LEN], -s[:, :CTX_LEN]], axis=0) / HY_N_CTX
  return dict(fwd=jnp.asarray(fwd, BF16), fwd_half=jnp.asarray(fwd[:CTX_LEN], BF16),
              inv=jnp.asarray(inv, BF16))


def _hy_positions(length):
  t_norm = jnp.linspace(0.0, 1.0, length, dtype=F32)[:, None]
  pos = jnp.arange(length, dtype=F32)[:, None]
  bands = jnp.linspace(1e-4, HY_BANDS - 1, HY_BANDS, dtype=F32)[None, :]
  ang = (2.0 * math.pi / length) * pos * bands
  feats = jnp.concatenate([t_norm, jnp.cos(ang), -jnp.sin(ang)], axis=-1)
  ext = lambda a: jnp.concatenate([a, a[:1], a[:0:-1]], axis=0)
  feats_t = jnp.pad(ext(feats).T, ((0, HY_FEAT_PAD - HY_FEAT), (0, 0)))
  return feats_t, ext(t_norm).T


def _hy_mlp_kernel(ft_ref, w1_ref, w2_ref, w3_ref, pv_ref, hi_ref, lo_ref):
  hp = lax.Precision.HIGHEST
  pv = pv_ref[...]
  h = jnp.sin(pv[:, 3:4] * (jnp.dot(w1_ref[...], ft_ref[...], precision=hp,
                                    preferred_element_type=F32) + pv[:, 0:1]))
  h = jnp.sin(pv[:, 4:5] * (jnp.dot(w2_ref[...], h, precision=hp,
                                    preferred_element_type=F32) + pv[:, 1:2]))
  h = jnp.sin(pv[:, 5:6] * (jnp.dot(w3_ref[...], h, precision=hp,
                                    preferred_element_type=F32) + pv[:, 2:3]))
  hi = h.astype(BF16)
  hi_ref[...] = hi
  lo_ref[...] = (h - hi.astype(F32)).astype(BF16)


def hy_mlp(feats_t, w1, b1, w2, b2, w3, b3, freq):
  n = feats_t.shape[1]
  tl = min(n, 2048)
  w1t = jnp.pad(w1.T, ((0, 0), (0, HY_FEAT_PAD - HY_FEAT)))
  pv = jnp.stack([b1, b2, b3, freq[0], freq[1], freq[2], b1 * 0, b1 * 0], axis=1)
  full = lambda shape: pl.BlockSpec(shape, lambda i: (0, 0))
  return pl.pallas_call(
      _hy_mlp_kernel,
      grid=(n // tl,),
      in_specs=[pl.BlockSpec((HY_FEAT_PAD, tl), lambda i: (0, i)), full((HY_HID, HY_FEAT_PAD)),
                full((HY_HID, HY_HID)), full((HY_HID, HY_HID)), full((HY_HID, 8))],
      out_specs=[pl.BlockSpec((HY_HID, tl), lambda i: (0, i))] * 2,
      out_shape=[jax.ShapeDtypeStruct((HY_HID, n), BF16)] * 2,
      compiler_params=_cparams(("parallel",)),
      name="hy_mlp",
  )(feats_t, w1t, w2.T, w3.T, pv)


def _hy_filter_kernel(hi_ref, lo_ref, wfh_ref, wfl_ref, wbh_ref, wbl_ref, tn_ref, dl_ref, o_ref):
  n = o_ref.shape[1]
  half = n // 2

  def part(wh_ref, wl_ref, sl):
    hh, hl, wh = hi_ref[:, sl], lo_ref[:, sl], wh_ref[...]
    return (jnp.dot(wh, hh, preferred_element_type=F32) + jnp.dot(wl_ref[...], hh, preferred_element_type=F32)
            + jnp.dot(wh, hl, preferred_element_type=F32))

  filt = jnp.concatenate([part(wfh_ref, wfl_ref, slice(0, half)),
                          part(wbh_ref, wbl_ref, slice(half, n))], axis=1)
  lane = lax.broadcasted_iota(jnp.int32, filt.shape, 1)
  filt = jnp.where(lane == half, 0.0, filt * jnp.exp(-tn_ref[...] * dl_ref[...]))
  o_ref[...] = filt * lax.rsqrt(jnp.sum(filt * filt, axis=-1, keepdims=True) + EPS)


def hy_filter(hdn_hi, hdn_lo, w4, tn_ext):
  n = hdn_hi.shape[1]
  c = w4.shape[1] // (2 * HY_ORDER)
  tc = 64
  w4t = w4.T.reshape(2, HY_ORDER, c, HY_HID)
  w4h = w4t.astype(BF16)
  w4l = (w4t - w4h.astype(F32)).astype(BF16)
  deltas = jnp.abs(jnp.linspace(math.log(HY_DECAY_TARGET) / HY_FAST_DECAY,
                                math.log(HY_DECAY_TARGET) / HY_SLOW_DECAY, c, dtype=F32)).reshape(c, 1)
  wspec = lambda d: pl.BlockSpec((None, None, tc, HY_HID), lambda o, i: (d, o, i, 0))
  hspec = pl.BlockSpec((HY_HID, n), lambda o, i: (0, 0))
  nct = c // tc
  return pl.pallas_call(
      _hy_filter_kernel,
      grid=(HY_ORDER, nct),
      in_specs=[hspec, hspec, wspec(0), wspec(0), wspec(1), wspec(1),
                pl.BlockSpec((1, n), lambda o, i: (0, 0)),
                pl.BlockSpec((tc, 1), lambda o, i: (i, 0))],
      out_specs=pl.BlockSpec((tc, n), lambda o, i: (o * nct + i, 0)),
      out_shape=jax.ShapeDtypeStruct((HY_ORDER * c, n), F32),
      compiler_params=_cparams(("parallel", "parallel")),
      name="hy_filter",
  )(hdn_hi, hdn_lo, w4h, w4l, w4h, w4l, tn_ext, deltas)


def _short_conv_cm(x, cw, period):
  width = x.shape[1]
  assert period & (period - 1) == 0
  t = jnp.bitwise_and(lax.broadcasted_iota(jnp.int32, x.shape, 1), period - 1)
  prev = jnp.where(t == 0, 0.0, pltpu.roll(x, 1, 1))
  nxt = jnp.where(t == period - 1, 0.0, pltpu.roll(x, width - 1, 1))
  return cw[:, 3:4] + cw[:, 0:1] * prev + cw[:, 1:2] * x + cw[:, 2:3] * nxt


def _hy_conv3_kernel(x_ref, cw_ref, o_ref):
  o_ref[...] = _short_conv_cm(x_ref[...], cw_ref[...], SEQ)


def hy_conv3_lat(hy_t, cw):
  rows = hy_t.shape[0]
  tc = 64
  return pl.pallas_call(
      _hy_conv3_kernel,
      grid=(rows // tc, BATCH),
      in_specs=[pl.BlockSpec((tc, SEQ), lambda i, b: (i, b)),
                pl.BlockSpec((tc, 4), lambda i, b: (i, 0))],
      out_specs=pl.BlockSpec((tc, SEQ), lambda i, b: (i, b)),
      out_shape=jax.ShapeDtypeStruct((rows, N_LAT), F32),
      compiler_params=_cparams(("parallel", "parallel")),
      name="hy_conv3",
  )(hy_t, cw)


def _dft_stage_a(x, fa, tc, ts):
  a = jnp.dot(fa, x.astype(BF16), preferred_element_type=F32)
  ar, ai = a[:DFT_R], a[DFT_R:]
  return jnp.concatenate([ar * tc + ai * ts, ai * tc - ar * ts], axis=1).astype(BF16)


def _hyena_lat_kernel(skip_ref, g0_ref, g1_ref, z_ref, k0_ref, k1_ref, fah_ref, faf_ref, tw_ref,
                      gc_ref, gci_ref, fai_ref, o_ref, h0_ref, h1_ref, *, n_ch):
  i = pl.program_id(0)
  b = pl.program_id(1)
  tc, ts = tw_ref[0], tw_ref[1]
  rows = HY_CG * DFT_R

  @pl.when(b == 0)
  def _():
    def spectrum(g, carry):
      base = g * HY_CG
      for k_ref, h_ref in ((k0_ref, h0_ref), (k1_ref, h1_ref)):
        ap = jnp.concatenate([_dft_stage_a(k_ref[base + c], faf_ref[...], tc, ts)
                              for c in range(HY_CG)], axis=0)
        h_ref[pl.ds(pl.multiple_of(g * rows, rows), rows), :] = jnp.dot(
            ap, gc_ref[...], preferred_element_type=F32)
      return carry
    lax.fori_loop(0, HY_CT // HY_CG, spectrum, 0)

  def group(g, carry):
    base = g * HY_CG
    zs = [z_ref[base + c] for c in range(HY_CG)]
    for o, (h_ref, gate_ref) in enumerate(((h0_ref, g0_ref), (h1_ref, g1_ref))):
      ap = jnp.concatenate([_dft_stage_a(zs[c], fah_ref[...], tc, ts) for c in range(HY_CG)], axis=0)
      x = jnp.dot(ap, gc_ref[...], preferred_element_type=F32)
      h = h_ref[pl.ds(pl.multiple_of(g * rows, rows), rows), :]
      xr, xi, hr, hi = x[:, :DFT_R], x[:, DFT_R:], h[:, :DFT_R], h[:, DFT_R:]
      y = jnp.concatenate([xr * hr - xi * hi, xr * hi + xi * hr], axis=1).astype(BF16)
      bb = jnp.dot(y, gci_ref[...], preferred_element_type=F32)
      nxt = []
      for c in range(HY_CG):
        bc = bb[c * DFT_R:(c + 1) * DFT_R]
        br, bi = bc[:, :DFT_R], bc[:, DFT_R:]
        st = jnp.concatenate([br * tc - bi * ts, br * ts + bi * tc], axis=0).astype(BF16)
        conv = jnp.dot(fai_ref[...], st, preferred_element_type=F32)
        sk = skip_ref[o * n_ch + i * HY_CT + base + c]
        nxt.append(gate_ref[base + c] * (conv + sk * zs[c]))
      zs = nxt
    for c in range(HY_CG):
      o_ref[base + c] = zs[c].astype(o_ref.dtype)
    return carry
  lax.fori_loop(0, HY_CT // HY_CG, group, 0)


def hyena_lat(u_t, kern_t, skip, consts):
  c = u_t.shape[0] // 3
  nct = c // HY_CT
  half = SEQ // DFT_R
  u3 = u_t.reshape(3 * c, N_LAT // DFT_R, DFT_R)
  k3 = kern_t.reshape(HY_ORDER * c, DFT_R, DFT_R)
  uspec = lambda part: pl.BlockSpec((HY_CT, half, DFT_R), lambda i, b, sk: (part * nct + i, b, 0))
  kspec = lambda o: pl.BlockSpec((HY_CT, DFT_R, DFT_R), lambda i, b, sk: (o * nct + i, 0, 0))
  cspec = lambda a: pl.BlockSpec(a.shape, lambda i, b, sk: (0,) * a.ndim)
  cs = [consts[k] for k in ("fa_half", "fa_full", "tw", "gc", "gci", "fai")]
  out = pl.pallas_call(
      functools.partial(_hyena_lat_kernel, n_ch=c),
      grid_spec=pltpu.PrefetchScalarGridSpec(
          num_scalar_prefetch=1,
          grid=(nct, BATCH),
          in_specs=[uspec(0), uspec(1), uspec(2), kspec(0), kspec(1)] + [cspec(a) for a in cs],
          out_specs=pl.BlockSpec((HY_CT, half, DFT_R), lambda i, b, sk: (i, b, 0)),
          scratch_shapes=[pltpu.VMEM((HY_CT * DFT_R, 2 * DFT_R), F32)] * 2,
      ),
      out_shape=jax.ShapeDtypeStruct((c, N_LAT // DFT_R, DFT_R), BF16),
      compiler_params=_cparams(("parallel", "arbitrary")),
      name="hyena_lat",
  )(skip.reshape(-1), u3, u3, u3, k3, k3, *cs)
  return out.reshape(c, N_LAT)


def _hyena_ctx_kernel(g0_ref, g1_ref, z_ref, cw0_ref, cw1_ref, cwz_ref, k0_ref, k1_ref, sk_ref,
                      fwd_ref, fwdh_ref, inv_ref, o_ref):
  n = HY_N_CTX
  g0 = _short_conv_cm(g0_ref[...], cw0_ref[...], CTX_LEN)
  g1 = _short_conv_cm(g1_ref[...], cw1_ref[...], CTX_LEN)
  z = _short_conv_cm(z_ref[...], cwz_ref[...], CTX_LEN)
  sk = sk_ref[...]
  spec = [jnp.dot(k_ref[...].astype(BF16), fwd_ref[...], preferred_element_type=F32)
          for k_ref in (k0_ref, k1_ref)]
  for b in range(BATCH):
    cols = slice(b * CTX_LEN, (b + 1) * CTX_LEN)
    zb = z[:, cols]
    for o, gate in enumerate((g0, g1)):
      x = jnp.dot(zb.astype(BF16), fwdh_ref[...], preferred_element_type=F32)
      h = spec[o]
      xr, xi, hr, hi = x[:, :n], x[:, n:], h[:, :n], h[:, n:]
      y = jnp.concatenate([xr * hr - xi * hi, xr * hi + xi * hr], axis=1).astype(BF16)
      conv = jnp.dot(y, inv_ref[...], preferred_element_type=F32)
      zb = gate[:, cols] * (conv + sk[:, o:o + 1] * zb)
    o_ref[:, cols] = zb.astype(o_ref.dtype)


def hyena_ctx(hy_t, cw, kern_t, skip, consts):
  c = hy_t.shape[0] // 3
  tc = 256 if c % 256 == 0 else c
  nct = c // tc
  blk = CTX_ROW0 // N_CTX
  xspec = lambda part: pl.BlockSpec((tc, N_CTX), lambda i: (part * nct + i, blk))
  wspec = lambda part: pl.BlockSpec((tc, 4), lambda i: (part * nct + i, 0))
  kspec = lambda o: pl.BlockSpec((tc, HY_N_CTX), lambda i: (o * nct + i, 0))
  cspec = lambda a: pl.BlockSpec(a.shape, lambda i: (0,) * a.ndim)
  cs = [consts[k] for k in ("fwd", "fwd_half", "inv")]
  return pl.pallas_call(
      _hyena_ctx_kernel,
      grid=(nct,),
      in_specs=[xspec(0), xspec(1), xspec(2), wspec(0), wspec(1), wspec(2), kspec(0), kspec(1),
                pl.BlockSpec((tc, HY_ORDER), lambda i: (i, 0))] + [cspec(a) for a in cs],
      out_specs=pl.BlockSpec((tc, N_CTX), lambda i: (i, 0)),
      out_shape=jax.ShapeDtypeStruct((c, N_CTX), BF16),
      compiler_params=_cparams(("parallel",)),
      name="hyena_ctx",
  )(hy_t, hy_t, hy_t, cw, cw, cw, kern_t, kern_t, skip.T, *cs)


def hyena_cm(hy_t, conv_w, conv_b, w1, b1, w2, b2, w3, b3, w4, freq, skip):
  c = hy_t.shape[0] // 3
  cw = jnp.concatenate([conv_w.T, conv_b[:, None]], axis=1)
  outs = []
  for length, lat in ((SEQ, True), (CTX_LEN, False)):
    feats_t, tn_ext = _hy_positions(length)
    hdn_hi, hdn_lo = hy_mlp(feats_t, w1, b1, w2, b2, w3, b3, freq)
    kern_t = hy_filter(hdn_hi, hdn_lo, w4, tn_ext)
    if lat:
      outs.append(hyena_lat(hy_conv3_lat(hy_t, cw), kern_t, skip, _dft_consts_lat()))
    else:
      outs.append(hyena_ctx(hy_t, cw, kern_t, skip, _dft_consts_ctx()))
  outs.append(jnp.zeros((c, T_PAD - T_REAL), BF16))
  return jnp.concatenate(outs, axis=1)


def _mm_nt_kernel(w_ref, x_ref, o_ref):
  o_ref[...] = lax.dot_general(w_ref[...], x_ref[...], (((1,), (1,)), ((), ())),
                               preferred_element_type=F32).astype(o_ref.dtype)


def mm_nt(w_t, x, tn=1024):
  n, k = w_t.shape
  return pl.pallas_call(
      _mm_nt_kernel,
      grid=(n // tn, T_PAD // TM),
      in_specs=[pl.BlockSpec((tn, k), lambda j, i: (j, 0)),
                pl.BlockSpec((TM, k), lambda j, i: (i, 0))],
      out_specs=pl.BlockSpec((tn, TM), lambda j, i: (j, i)),
      out_shape=jax.ShapeDtypeStruct((n, T_PAD), F32),
      compiler_params=_cparams(("parallel", "parallel")),
      name="mm_nt",
  )(w_t, x)


def _mm_resid2_kernel(a_ref, zt_ref, wa_ref, wz_ref, s_ref, m_ref, o_ref):
  acc = jnp.dot(a_ref[...], wa_ref[...], preferred_element_type=F32)
  acc += lax.dot_general(zt_ref[...], wz_ref[...], (((0,), (0,)), ((), ())),
                         preferred_element_type=F32)
  o_ref[...] = s_ref[...] + m_ref[...] * acc


def mm_resid2(a, z_t, w, s, gate8, tn=512):
  ka, kz = a.shape[1], z_t.shape[0]
  n = w.shape[1]
  assert ka == kz
  row = _mod_row(TM)
  return pl.pallas_call(
      _mm_resid2_kernel,
      grid=(n // tn, T_PAD // TM),
      in_specs=[
          pl.BlockSpec((TM, ka), lambda j, i: (i, 0)),
          pl.BlockSpec((kz, TM), lambda j, i: (0, i)),
          pl.BlockSpec((ka, tn), lambda j, i: (0, j)),
          pl.BlockSpec((kz, tn), lambda j, i: (1, j)),
          pl.BlockSpec((TM, tn), lambda j, i: (i, j)),
          pl.BlockSpec((None, 1, tn), lambda j, i: (row(i), 0, j)),
      ],
      out_specs=pl.BlockSpec((TM, tn), lambda j, i: (i, j)),
      out_shape=jax.ShapeDtypeStruct((T_PAD, n), F32),
      input_output_aliases={4: 0},
      compiler_params=_cparams(("parallel", "parallel")),
      name="mm_resid2",
  )(a, z_t, w, w, s, gate8)


def kernel(x, c, ctx, c_ctx, ada_a, ada_b, ada_bias, norm_mix, norm_ffn, even_w_in, na_q_gain,
           na_k_gain, na_rpb, hy_conv_w, hy_conv_b, hy_w1, hy_b1, hy_w2, hy_b2, hy_w3, hy_b3,
           hy_w4, hy_freq, hy_skip, odd_w_in, odd_b_in, gm_v_gain, gm_w_s, gm_b_s, w_mix_out,
           moe_router, moe_bias, moe_w_gate, moe_w_up, moe_w_down, sh_w_gate, sh_w_up, sh_w_down):
  s = jnp.concatenate([x.reshape(N_LAT, D_MODEL), ctx.reshape(N_CTX, D_MODEL),
                       jnp.zeros((T_PAD - T_REAL, D_MODEL), F32)], axis=0)
  cond8 = jnp.concatenate([c, c_ctx[None], jnp.zeros((8 - BATCH - 1, D_MODEL), F32)], axis=0)
  mods = ada_table(cond8, ada_a, ada_b, ada_bias).reshape(DEPTH, 8, N_MOD, 1, D_MODEL)

  for layer in range(DEPTH):
    j = layer // 2
    mod = [mods[layer, :, i] for i in range(N_MOD)]
    h = norm_mod(s, norm_mix[layer], mod[0], mod[1])
    w_out = w_mix_out[layer].astype(BF16)
    if layer % 2 == 0:
      w_in = even_w_in[j]
      gains = jnp.concatenate([jnp.tile(na_q_gain[j], NA_HEADS), jnp.tile(na_k_gain[j], NA_HEADS),
                               jnp.ones((NA_WIDTH,), F32)]).reshape(1, 3 * NA_WIDTH)
      qkv = mm_headnorm(h, w_in[:, :3 * NA_WIDTH].astype(BF16), gains, 2 * NA_WIDTH)
      hy_t = mm_nt(w_in[:, 3 * NA_WIDTH:].T.astype(BF16), h)
      a = jnp.concatenate([na_attention(qkv, na_bias(na_rpb[j])), ctx_attention(qkv),
                           jnp.zeros((T_PAD - T_REAL, NA_WIDTH), BF16)], axis=0)
      z_t = hyena_cm(hy_t, hy_conv_w[j], hy_conv_b[j], hy_w1[j], hy_b1[j], hy_w2[j], hy_b2[j],
                     hy_w3[j], hy_b3[j], hy_w4[j], hy_freq[j], hy_skip[j])
      s = mm_resid2(a, z_t, w_out, s, mod[2])
    else:
      z = mm_gelu(h, odd_w_in[j].astype(BF16), odd_b_in[j])
      bsx = jnp.repeat(gm_b_s[j].T, HEAD_DIM, axis=1)
      mix = gmlp_gate(z, gm_v_gain[j], gm_w_s[j].astype(BF16), bsx)
      s = mm_resid(mix, w_out, s, mod[2])

    f, gates = norm_mod_router(s, norm_ffn[layer], mod[3], mod[4], moe_router[layer], moe_bias[layer])
    gates_t = gates[:, :N_GATE_COLS].reshape(T_PAD, N_GATE_COLS // EXP_PER_TILE, EXP_PER_TILE)
    gates_t = gates_t.transpose(1, 0, 2)
    wg = jnp.concatenate([moe_w_gate[layer].transpose(1, 0, 2).reshape(D_MODEL, -1),
                          sh_w_gate[layer]], axis=1).astype(BF16)
    wu = jnp.concatenate([moe_w_up[layer].transpose(1, 0, 2).reshape(D_MODEL, -1),
                          sh_w_up[layer]], axis=1).astype(BF16)
    wd = jnp.concatenate([moe_w_down[layer].reshape(-1, D_MODEL), sh_w_down[layer]],
                         axis=0).astype(BF16)
    hid = moe_up(f, wg, wu, gates_t)
    s = mm_resid_k(hid, wd, s, mod[5])

  return s[:N_LAT].reshape(BATCH, SEQ, D_MODEL)
```

```python
import functools
import math

import numpy as np
import jax
import jax.numpy as jnp
from jax import lax
from jax.experimental import pallas as pl
from jax.experimental.pallas import tpu as pltpu

F32 = jnp.float32
BF16 = jnp.bfloat16

D_MODEL = 4096
BATCH = 2
SEQ = 8192
DEPTH = 4
GRID_W = 64
ROWS = SEQ // GRID_W
CTX_LEN = 256
HEAD_DIM = 128
NA_HEADS = 16
NA_WIDTH = NA_HEADS * HEAD_DIM
HY_WIDTH = D_MODEL - NA_WIDTH
WIN_H = 8
WIN_W = 16
HY_ORDER = 2
HY_SHORT = 3
HY_BANDS = 16
HY_DECAY_TARGET = 1e-2
HY_FAST_DECAY = 0.3
HY_SLOW_DECAY = 1.5
GM_WIDTH = D_MODEL
GM_GROUPS = GM_WIDTH // HEAD_DIM
CHUNK = 128
N_EXPERTS = 64
TOP_K = 8
EXPERT_DIM = 128
SHARED_DIM = TOP_K * EXPERT_DIM
ROUTED_SCALE = 2.5
N_MOD = 6
EPS = 1e-6

LANES = 128
VMEM_LIMIT = 56 * 1024 * 1024

TM = 1024
N_LAT = BATCH * SEQ
N_CTX = BATCH * CTX_LEN
T_REAL = N_LAT + N_CTX
T_PAD = -(-T_REAL // TM) * TM
CTX_ROW0 = N_LAT
TN_NORM = 512
MOE_TN = 512
EXP_PER_TILE = MOE_TN // EXPERT_DIM
NEG = -1e30

NA_R = 4
NA_Q = NA_R * GRID_W
NA_KROWS = NA_R + WIN_H
NA_KTOK = NA_KROWS * GRID_W
NA_QBLOCKS = ROWS // NA_R


def _mod_row(tile_rows):
  def f(i):
    return jnp.minimum((i * tile_rows) // SEQ, BATCH)
  return f


def _cparams(sem):
  return pltpu.CompilerParams(dimension_semantics=sem, vmem_limit_bytes=VMEM_LIMIT)


def _ada_kernel(cond_ref, a_ref, b_ref, bias_ref, o_ref):
  cnd = cond_ref[...]
  sc = cnd * jax.nn.sigmoid(cnd)
  t = jnp.dot(sc, a_ref[...], preferred_element_type=F32, precision=lax.Precision.HIGHEST)
  o_ref[...] = jnp.dot(t, b_ref[...], preferred_element_type=F32,
                       precision=lax.Precision.HIGHEST) + bias_ref[...]


def ada_table(cond8, ada_a, ada_b, ada_bias):
  tn = 4096
  rank = ada_a.shape[-1]
  nmd = N_MOD * D_MODEL
  return pl.pallas_call(
      _ada_kernel,
      grid=(DEPTH, nmd // tn),
      in_specs=[
          pl.BlockSpec((8, D_MODEL), lambda l, j: (0, 0)),
          pl.BlockSpec((None, D_MODEL, rank), lambda l, j: (l, 0, 0)),
          pl.BlockSpec((None, rank, tn), lambda l, j: (l, 0, j)),
          pl.BlockSpec((None, 1, tn), lambda l, j: (l, 0, j)),
      ],
      out_specs=pl.BlockSpec((None, 8, tn), lambda l, j: (l, 0, j)),
      out_shape=jax.ShapeDtypeStruct((DEPTH, 8, nmd), F32),
      compiler_params=_cparams(("arbitrary", "arbitrary")),
      name="ada_table",
  )(cond8, ada_a, ada_b, ada_bias.reshape(DEPTH, 1, nmd))


def _norm_mod(x, g, shift, scale):
  ms = jnp.mean(x * x, axis=-1, keepdims=True)
  y = x * lax.rsqrt(ms + EPS) * g
  return y * (1.0 + scale) + shift


def _norm_kernel(x_ref, g_ref, shift_ref, scale_ref, o_ref):
  o_ref[...] = _norm_mod(x_ref[...], g_ref[...], shift_ref[...], scale_ref[...]).astype(BF16)


def _norm_router_kernel(x_ref, g_ref, shift_ref, scale_ref, whi_ref, wlo_ref, rb_ref,
                        o_ref, gates_ref):
  f = _norm_mod(x_ref[...], g_ref[...], shift_ref[...], scale_ref[...])
  f_hi = f.astype(BF16)
  o_ref[...] = f_hi
  f_lo = (f - f_hi.astype(F32)).astype(BF16)
  w_hi = whi_ref[...]
  logits = (jnp.dot(f_hi, w_hi, preferred_element_type=F32)
            + jnp.dot(f_lo, w_hi, preferred_element_type=F32)
            + jnp.dot(f_hi, wlo_ref[...], preferred_element_type=F32))
  scores = jax.nn.sigmoid(logits)
  lane = lax.broadcasted_iota(jnp.int32, scores.shape, 1).astype(F32)
  work = jnp.where(lane < N_EXPERTS, scores + rb_ref[...], NEG)
  sel = jnp.zeros(scores.shape, F32)
  for _ in range(TOP_K):
    m = jnp.max(work, axis=-1, keepdims=True)
    first = jnp.min(jnp.where(work == m, lane, float(LANES)), axis=-1, keepdims=True)
    hit = lane == first
    sel = jnp.where(hit, scores, sel)
    work = jnp.where(hit, 3.0 * NEG, work)
  gates_ref[...] = sel / jnp.sum(sel, axis=-1, keepdims=True) * ROUTED_SCALE


def norm_mod(s, gain, shift8, scale8):
  n = T_PAD // TN_NORM
  row = _mod_row(TN_NORM)
  return pl.pallas_call(
      _norm_kernel,
      grid=(n,),
      in_specs=[
          pl.BlockSpec((TN_NORM, D_MODEL), lambda i: (i, 0)),
          pl.BlockSpec((1, D_MODEL), lambda i: (0, 0)),
          pl.BlockSpec((None, 1, D_MODEL), lambda i: (row(i), 0, 0)),
          pl.BlockSpec((None, 1, D_MODEL), lambda i: (row(i), 0, 0)),
      ],
      out_specs=pl.BlockSpec((TN_NORM, D_MODEL), lambda i: (i, 0)),
      out_shape=jax.ShapeDtypeStruct((T_PAD, D_MODEL), BF16),
      compiler_params=_cparams(("parallel",)),
      name="norm_mod",
  )(s, gain.reshape(1, D_MODEL), shift8, scale8)


def norm_mod_router(s, gain, shift8, scale8, w_router, r_bias):
  n = T_PAD // TN_NORM
  row = _mod_row(TN_NORM)
  w_pad = jnp.pad(w_router, ((0, 0), (0, LANES - N_EXPERTS)))
  w_hi = w_pad.astype(BF16)
  w_lo = (w_pad - w_hi.astype(F32)).astype(BF16)
  rb = jnp.pad(r_bias.astype(F32), (0, LANES - N_EXPERTS)).reshape(1, LANES)
  return pl.pallas_call(
      _norm_router_kernel,
      grid=(n,),
      in_specs=[
          pl.BlockSpec((TN_NORM, D_MODEL), lambda i: (i, 0)),
          pl.BlockSpec((1, D_MODEL), lambda i: (0, 0)),
          pl.BlockSpec((None, 1, D_MODEL), lambda i: (row(i), 0, 0)),
          pl.BlockSpec((None, 1, D_MODEL), lambda i: (row(i), 0, 0)),
          pl.BlockSpec((D_MODEL, LANES), lambda i: (0, 0)),
          pl.BlockSpec((D_MODEL, LANES), lambda i: (0, 0)),
          pl.BlockSpec((1, LANES), lambda i: (0, 0)),
      ],
      out_specs=[
          pl.BlockSpec((TN_NORM, D_MODEL), lambda i: (i, 0)),
          pl.BlockSpec((TN_NORM, LANES), lambda i: (i, 0)),
      ],
      out_shape=[
          jax.ShapeDtypeStruct((T_PAD, D_MODEL), BF16),
          jax.ShapeDtypeStruct((T_PAD, LANES), F32),
      ],
      compiler_params=_cparams(("parallel",)),
      name="norm_mod_router",
  )(s, gain.reshape(1, D_MODEL), shift8, scale8, w_hi, w_lo, rb)


def _gelu_exact(x):
  return 0.5 * x * (1.0 + lax.erf(x * (1.0 / math.sqrt(2.0))))


def _cast_weight_once(w_ref, wb_ref):
  @pl.when(pl.program_id(1) == 0)
  def _():
    wb_ref[...] = w_ref[...].astype(BF16)


def _mm_headnorm_kernel(x_ref, w_ref, g_ref, o_ref, wb_ref, *, n_norm_tiles):
  _cast_weight_once(w_ref, wb_ref)
  acc = jnp.dot(x_ref[...], wb_ref[...], preferred_element_type=F32)
  j = pl.program_id(0)

  @pl.when(j < n_norm_tiles)
  def _():
    for h in range(acc.shape[1] // HEAD_DIM):
      blk = acc[:, h * HEAD_DIM:(h + 1) * HEAD_DIM]
      ms = jnp.mean(blk * blk, axis=-1, keepdims=True)
      y = blk * lax.rsqrt(ms + EPS) * g_ref[:, h * HEAD_DIM:(h + 1) * HEAD_DIM]
      o_ref[:, h * HEAD_DIM:(h + 1) * HEAD_DIM] = y.astype(o_ref.dtype)

  @pl.when(j >= n_norm_tiles)
  def _():
    o_ref[...] = acc.astype(o_ref.dtype)


def _mm_gelu_kernel(x_ref, w_ref, b_ref, o_ref, wb_ref):
  _cast_weight_once(w_ref, wb_ref)
  acc = jnp.dot(x_ref[...], wb_ref[...], preferred_element_type=F32) + b_ref[...]
  o_ref[...] = _gelu_exact(acc).astype(o_ref.dtype)


def _mm_resid_kernel(x_ref, w_ref, s_ref, m_ref, o_ref, wb_ref):
  _cast_weight_once(w_ref, wb_ref)
  acc = jnp.dot(x_ref[...], wb_ref[...], preferred_element_type=F32)
  o_ref[...] = s_ref[...] + m_ref[...] * acc


def _moe_down_kernel(hr_ref, hs_ref, wr_ref, ws_ref, s_ref, m_ref, o_ref, acc_ref):
  k = pl.program_id(2)
  last = pl.num_programs(2) - 1

  @pl.when(k == 0)
  def _():
    acc_ref[...] = jnp.zeros_like(acc_ref)

  @pl.when(k < last)
  def _():
    acc_ref[...] += jnp.dot(hr_ref[...], wr_ref[...], preferred_element_type=F32)

  @pl.when(k == last)
  def _():
    acc = acc_ref[...] + jnp.dot(hs_ref[...], ws_ref[...], preferred_element_type=F32)
    o_ref[...] = s_ref[...] + m_ref[...] * acc


def _swiglu(x, wg, wu):
  g = jnp.dot(x, wg, preferred_element_type=F32)
  u = jnp.dot(x, wu, preferred_element_type=F32)
  return g * jax.nn.sigmoid(g) * u


def _moe_up_kernel(x_ref, wg_ref, wu_ref, gate_ref, o_ref, wgb_ref, wub_ref):
  @pl.when(pl.program_id(1) == 0)
  def _():
    for e in range(EXP_PER_TILE):
      sl = slice(e * EXPERT_DIM, (e + 1) * EXPERT_DIM)
      wgb_ref[:, sl] = wg_ref[e].astype(BF16)
      wub_ref[:, sl] = wu_ref[e].astype(BF16)

  hid = _swiglu(x_ref[...], wgb_ref[...], wub_ref[...])
  gate = gate_ref[...]
  for e in range(EXP_PER_TILE):
    sl = slice(e * EXPERT_DIM, (e + 1) * EXPERT_DIM)
    o_ref[:, sl] = (hid[:, sl] * gate[:, e:e + 1]).astype(o_ref.dtype)


def _moe_up_shared_kernel(x_ref, wg_ref, wu_ref, o_ref, wgb_ref, wub_ref):
  _cast_weight_once(wg_ref, wgb_ref)
  _cast_weight_once(wu_ref, wub_ref)
  o_ref[...] = _swiglu(x_ref[...], wgb_ref[...], wub_ref[...]).astype(o_ref.dtype)


def _xspec(k):
  return pl.BlockSpec((TM, k), lambda j, i: (i, 0))


def _wspec(k, tn, col0=0):
  return pl.BlockSpec((k, tn), lambda j, i: (0, col0 // tn + j), pipeline_mode=pl.Buffered(1))


def mm_headnorm(x, w, gains, n, n_norm_cols, tn=1024):
  k = w.shape[0]
  return pl.pallas_call(
      functools.partial(_mm_headnorm_kernel, n_norm_tiles=n_norm_cols // tn),
      grid=(n // tn, T_PAD // TM),
      in_specs=[_xspec(k), _wspec(k, tn), pl.BlockSpec((1, tn), lambda j, i: (0, j))],
      out_specs=pl.BlockSpec((TM, tn), lambda j, i: (i, j)),
      out_shape=jax.ShapeDtypeStruct((T_PAD, n), BF16),
      scratch_shapes=[pltpu.VMEM((k, tn), BF16)],
      compiler_params=_cparams(("parallel", "arbitrary")),
      name="mm_headnorm",
  )(x, w, gains)


def mm_gelu(x, w, b, tn=1024):
  k, n = w.shape
  return pl.pallas_call(
      _mm_gelu_kernel,
      grid=(n // tn, T_PAD // TM),
      in_specs=[_xspec(k), _wspec(k, tn), pl.BlockSpec((1, tn), lambda j, i: (0, j))],
      out_specs=pl.BlockSpec((TM, tn), lambda j, i: (i, j)),
      out_shape=jax.ShapeDtypeStruct((T_PAD, n), BF16),
      scratch_shapes=[pltpu.VMEM((k, tn), BF16)],
      compiler_params=_cparams(("parallel", "arbitrary")),
      name="mm_gelu",
  )(x, w, b.reshape(1, n))


def mm_resid(x, w, s, gate8, tn=512):
  k, n = w.shape
  row = _mod_row(TM)
  return pl.pallas_call(
      _mm_resid_kernel,
      grid=(n // tn, T_PAD // TM),
      in_specs=[_xspec(k), _wspec(k, tn),
                pl.BlockSpec((TM, tn), lambda j, i: (i, j)),
                pl.BlockSpec((None, 1, tn), lambda j, i: (row(i), 0, j))],
      out_specs=pl.BlockSpec((TM, tn), lambda j, i: (i, j)),
      out_shape=jax.ShapeDtypeStruct((T_PAD, n), F32),
      scratch_shapes=[pltpu.VMEM((k, tn), BF16)],
      input_output_aliases={2: 0},
      compiler_params=_cparams(("parallel", "arbitrary")),
      name="mm_resid",
  )(x, w, s, gate8)


def moe_down(hid_r, hid_s, wd_r, wd_s, s, gate8, tn=1024, tk=1024):
  kr, n = wd_r.shape
  ks = wd_s.shape[0]
  assert ks == tk and kr % tk == 0
  nk = kr // tk
  row = _mod_row(TM)
  return pl.pallas_call(
      _moe_down_kernel,
      grid=(n // tn, T_PAD // TM, nk + 1),
      in_specs=[
          pl.BlockSpec((TM, tk), lambda j, i, kk: (i, jnp.minimum(kk, nk - 1))),
          pl.BlockSpec((TM, tk), lambda j, i, kk: (i, 0)),
          pl.BlockSpec((tk, tn), lambda j, i, kk: (jnp.minimum(kk, nk - 1), j)),
          pl.BlockSpec((tk, tn), lambda j, i, kk: (0, j)),
          pl.BlockSpec((TM, tn), lambda j, i, kk: (i, j)),
          pl.BlockSpec((None, 1, tn), lambda j, i, kk: (row(i), 0, j)),
      ],
      out_specs=pl.BlockSpec((TM, tn), lambda j, i, kk: (i, j)),
      out_shape=jax.ShapeDtypeStruct((T_PAD, n), F32),
      scratch_shapes=[pltpu.VMEM((TM, tn), F32)],
      input_output_aliases={4: 0},
      compiler_params=_cparams(("parallel", "parallel", "arbitrary")),
      name="moe_down",
  )(hid_r, hid_s, wd_r, wd_s, s, gate8)


def moe_up(x, w_gate, w_up, gates_t):
  n_exp, k, f = w_gate.shape
  tn = MOE_TN
  wspec = pl.BlockSpec((EXP_PER_TILE, k, f), lambda j, i: (j, 0, 0), pipeline_mode=pl.Buffered(1))
  return pl.pallas_call(
      _moe_up_kernel,
      grid=(n_exp // EXP_PER_TILE, T_PAD // TM),
      in_specs=[_xspec(k), wspec, wspec,
                pl.BlockSpec((None, TM, EXP_PER_TILE), lambda j, i: (j, i, 0))],
      out_specs=pl.BlockSpec((TM, tn), lambda j, i: (i, j)),
      out_shape=jax.ShapeDtypeStruct((T_PAD, n_exp * f), BF16),
      scratch_shapes=[pltpu.VMEM((k, tn), BF16)] * 2,
      compiler_params=_cparams(("parallel", "arbitrary")),
      name="moe_up",
  )(x, w_gate, w_up, gates_t)


def moe_up_shared(x, wg, wu):
  k, n = wg.shape
  tn = MOE_TN
  return pl.pallas_call(
      _moe_up_shared_kernel,
      grid=(n // tn, T_PAD // TM),
      in_specs=[_xspec(k), _wspec(k, tn), _wspec(k, tn)],
      out_specs=pl.BlockSpec((TM, tn), lambda j, i: (i, j)),
      out_shape=jax.ShapeDtypeStruct((T_PAD, n), BF16),
      scratch_shapes=[pltpu.VMEM((k, tn), BF16)] * 2,
      compiler_params=_cparams(("parallel", "arbitrary")),
      name="moe_up_shared",
  )(x, wg, wu)


def _na_kernel(q_ref, k_ref, v_ref, kc_ref, vc_ref, bias_ref, o_ref):
  qb = pl.program_id(2)
  kb = jnp.clip(qb * NA_R - WIN_H // 2, 0, ROWS - NA_KROWS)
  start = pl.multiple_of(kb * GRID_W, GRID_W)
  kind = jnp.where(qb == 0, 0, jnp.where(qb == NA_QBLOCKS - 1, 2, 1))
  scale = HEAD_DIM ** -0.5
  nt = (((1,), (1,)), ((), ()))
  q = q_ref[...]
  kw = k_ref[pl.ds(start, NA_KTOK), :]
  vw = v_ref[pl.ds(start, NA_KTOK), :]
  s = lax.dot_general(q, kw, nt, preferred_element_type=F32) * scale + bias_ref[kind]
  sc = lax.dot_general(q, kc_ref[...], nt, preferred_element_type=F32) * scale
  m = jnp.maximum(jnp.max(s, axis=-1, keepdims=True), jnp.max(sc, axis=-1, keepdims=True))
  p = jnp.exp(s - m)
  pc = jnp.exp(sc - m)
  den = jnp.sum(p, axis=-1, keepdims=True) + jnp.sum(pc, axis=-1, keepdims=True)
  o = (jnp.dot(p.astype(BF16), vw, preferred_element_type=F32)
       + jnp.dot(pc.astype(BF16), vc_ref[...], preferred_element_type=F32))
  o_ref[...] = (o / den).astype(o_ref.dtype)


def _na_bias_selectors():
  rows = np.zeros((3, NA_R, NA_KROWS, 2 * WIN_H - 1), np.float32)
  for kind, qb in enumerate((0, 1, NA_QBLOCKS - 1)):
    r0 = qb * NA_R
    kb = int(np.clip(r0 - WIN_H // 2, 0, ROWS - NA_KROWS))
    for i in range(NA_R):
      r = r0 + i
      rs = int(np.clip(r - WIN_H // 2, 0, ROWS - WIN_H))
      for j in range(NA_KROWS):
        if rs <= kb + j < rs + WIN_H:
          rows[kind, i, j, kb + j - r + WIN_H - 1] = 1.0
  cols = np.zeros((2 * WIN_W - 1, GRID_W, GRID_W), np.float32)
  for c in range(GRID_W):
    cs = int(np.clip(c - WIN_W // 2, 0, GRID_W - WIN_W))
    for d in range(cs, cs + WIN_W):
      cols[d - c + WIN_W - 1, c, d] = 1.0
  return rows, cols


def na_bias(rpb):
  rows, cols = _na_bias_selectors()
  hp = lax.Precision.HIGHEST
  colband = jnp.einsum('hrw,wcd->hrcd', rpb.astype(F32), cols, precision=hp)
  band = jnp.einsum('tijr,hrcd->thicjd', rows, colband, precision=hp)
  valid = np.einsum('tijr,wcd->ticjd', rows, cols) > 0.5
  band = jnp.where(valid[:, None], band, NEG)
  return band.reshape(3, NA_HEADS, NA_Q, NA_KTOK)


def na_attention(qkv, bias):
  hq = NA_HEADS
  lat_blocks = SEQ // NA_Q
  ctx_blk0 = CTX_ROW0 // CTX_LEN
  return pl.pallas_call(
      _na_kernel,
      grid=(BATCH, NA_HEADS, NA_QBLOCKS),
      in_specs=[
          pl.BlockSpec((NA_Q, HEAD_DIM), lambda b, h, i: (b * lat_blocks + i, h)),
          pl.BlockSpec((SEQ, HEAD_DIM), lambda b, h, i: (b, hq + h)),
          pl.BlockSpec((SEQ, HEAD_DIM), lambda b, h, i: (b, 2 * hq + h)),
          pl.BlockSpec((CTX_LEN, HEAD_DIM), lambda b, h, i: (ctx_blk0 + b, hq + h)),
          pl.BlockSpec((CTX_LEN, HEAD_DIM), lambda b, h, i: (ctx_blk0 + b, 2 * hq + h)),
          pl.BlockSpec((3, None, NA_Q, NA_KTOK), lambda b, h, i: (0, h, 0, 0)),
      ],
      out_specs=pl.BlockSpec((NA_Q, HEAD_DIM), lambda b, h, i: (b * lat_blocks + i, h)),
      out_shape=jax.ShapeDtypeStruct((N_LAT, NA_WIDTH), BF16),
      compiler_params=_cparams(("parallel", "parallel", "arbitrary")),
      name="na_attention",
  )(qkv, qkv, qkv, qkv, qkv, bias)


def _ctx_attn_kernel(q_ref, k_ref, v_ref, o_ref):
  scale = HEAD_DIM ** -0.5
  s = lax.dot_general(q_ref[...], k_ref[...], (((1,), (1,)), ((), ())),
                      preferred_element_type=F32) * scale
  m = jnp.max(s, axis=-1, keepdims=True)
  p = jnp.exp(s - m)
  den = jnp.sum(p, axis=-1, keepdims=True)
  o = jnp.dot(p.astype(BF16), v_ref[...], preferred_element_type=F32)
  o_ref[...] = (o / den).astype(o_ref.dtype)


def ctx_attention(qkv):
  hq = NA_HEADS
  ctx_blk0 = CTX_ROW0 // CTX_LEN
  return pl.pallas_call(
      _ctx_attn_kernel,
      grid=(BATCH, NA_HEADS),
      in_specs=[
          pl.BlockSpec((CTX_LEN, HEAD_DIM), lambda b, h: (ctx_blk0 + b, h)),
          pl.BlockSpec((CTX_LEN, HEAD_DIM), lambda b, h: (ctx_blk0 + b, hq + h)),
          pl.BlockSpec((CTX_LEN, HEAD_DIM), lambda b, h: (ctx_blk0 + b, 2 * hq + h)),
      ],
      out_specs=pl.BlockSpec((CTX_LEN, HEAD_DIM), lambda b, h: (b, h)),
      out_shape=jax.ShapeDtypeStruct((N_CTX, NA_WIDTH), BF16),
      compiler_params=_cparams(("parallel", "parallel")),
      name="ctx_attention",
  )(qkv, qkv, qkv)


def _gmlp_gate_kernel(u_ref, v_ref, vg_ref, ws_ref, bs_ref, o_ref):
  v = v_ref[...].astype(F32)
  ms = jnp.mean(v * v, axis=-1, keepdims=True)
  vn = (v * lax.rsqrt(ms + EPS) * vg_ref[...]).astype(BF16)
  for c in range(v.shape[0] // CHUNK):
    rows = slice(c * CHUNK, (c + 1) * CHUNK)
    for g in range(GM_GROUPS):
      cols = slice(g * HEAD_DIM, (g + 1) * HEAD_DIM)
      sp = jnp.dot(ws_ref[g], vn[rows, cols], preferred_element_type=F32) + bs_ref[:, cols]
      o_ref[rows, cols] = (u_ref[rows, cols].astype(F32) * sp).astype(o_ref.dtype)


def gmlp_gate(z, v_gain, w_s, bsx):
  tg = 256
  ncol = GM_WIDTH // GM_WIDTH
  return pl.pallas_call(
      _gmlp_gate_kernel,
      grid=(T_PAD // tg,),
      in_specs=[
          pl.BlockSpec((tg, GM_WIDTH), lambda i: (i, 0)),
          pl.BlockSpec((tg, GM_WIDTH), lambda i: (i, ncol)),
          pl.BlockSpec((1, GM_WIDTH), lambda i: (0, 0)),
          pl.BlockSpec((GM_GROUPS, CHUNK, CHUNK), lambda i: (0, 0, 0)),
          pl.BlockSpec((CHUNK, GM_WIDTH), lambda i: (0, 0)),
      ],
      out_specs=pl.BlockSpec((tg, GM_WIDTH), lambda i: (i, 0)),
      out_shape=jax.ShapeDtypeStruct((T_PAD, GM_WIDTH), BF16),
      compiler_params=_cparams(("parallel",)),
      name="gmlp_gate",
  )(z, z, v_gain.reshape(1, GM_WIDTH), w_s, bsx)


HY_FEAT = 1 + 2 * HY_BANDS
HY_FEAT_PAD = 40
HY_HID = 64
DFT_R = 128
HY_N_LAT = 2 * SEQ
HY_N_CTX = 2 * CTX_LEN
HY_CT = 32
HY_CG = 8
assert HY_N_LAT == DFT_R * DFT_R


def _dft_consts_lat():
  n = np.arange(DFT_R)
  ang = 2.0 * np.pi * np.outer(n, n) / DFT_R
  c, s = np.cos(ang), np.sin(ang)
  tw = 2.0 * np.pi * np.outer(n, n) / HY_N_LAT
  half = SEQ // DFT_R
  fa = np.concatenate([c, -s], axis=0)
  gc = np.block([[c, -s], [s, c]])
  gci = np.block([[c, s], [-s, c]])
  fai = np.concatenate([c[:half], -s[:half]], axis=1) / HY_N_LAT
  return dict(
      fa_full=jnp.asarray(fa, BF16), fa_half=jnp.asarray(fa[:, :half], BF16),
      tw=jnp.asarray(np.stack([np.cos(tw), np.sin(tw)]), F32),
      gc=jnp.asarray(gc, BF16), gci=jnp.asarray(gci, BF16), fai=jnp.asarray(fai, BF16))


def _dft_consts_ctx():
  n = np.arange(HY_N_CTX)
  ang = 2.0 * np.pi * np.outer(n, n) / HY_N_CTX
  c, s = np.cos(ang), np.sin(ang)
  fwd = np.concatenate([c, -s], axis=1)
  inv = np.concatenate([c[:, :CTX_LEN], -s[:, :CTX_LEN]], axis=0) / HY_N_CTX
  return dict(fwd=jnp.asarray(fwd, BF16), fwd_half=jnp.asarray(fwd[:CTX_LEN], BF16),
              inv=jnp.asarray(inv, BF16))


def _hy_positions(length):
  t_norm = jnp.linspace(0.0, 1.0, length, dtype=F32)[:, None]
  pos = jnp.arange(length, dtype=F32)[:, None]
  bands = jnp.linspace(1e-4, HY_BANDS - 1, HY_BANDS, dtype=F32)[None, :]
  ang = (2.0 * math.pi / length) * pos * bands
  feats = jnp.concatenate([t_norm, jnp.cos(ang), -jnp.sin(ang)], axis=-1)
  ext = lambda a: jnp.concatenate([a, a[:1], a[:0:-1]], axis=0)
  feats_t = jnp.pad(ext(feats).T, ((0, HY_FEAT_PAD - HY_FEAT), (0, 0)))
  return feats_t, ext(t_norm).T


def _hy_mlp_kernel(ft_ref, w1_ref, w2_ref, w3_ref, pv_ref, hi_ref, lo_ref):
  hp = lax.Precision.HIGHEST
  pv = pv_ref[...]
  h = jnp.sin(pv[:, 3:4] * (jnp.dot(w1_ref[...], ft_ref[...], precision=hp,
                                    preferred_element_type=F32) + pv[:, 0:1]))
  h = jnp.sin(pv[:, 4:5] * (jnp.dot(w2_ref[...], h, precision=hp,
                                    preferred_element_type=F32) + pv[:, 1:2]))
  h = jnp.sin(pv[:, 5:6] * (jnp.dot(w3_ref[...], h, precision=hp,
                                    preferred_element_type=F32) + pv[:, 2:3]))
  hi = h.astype(BF16)
  hi_ref[...] = hi
  lo_ref[...] = (h - hi.astype(F32)).astype(BF16)


def hy_mlp(feats_t, w1, b1, w2, b2, w3, b3, freq):
  n = feats_t.shape[1]
  tl = min(n, 2048)
  w1t = jnp.pad(w1.T, ((0, 0), (0, HY_FEAT_PAD - HY_FEAT)))
  pv = jnp.stack([b1, b2, b3, freq[0], freq[1], freq[2], b1 * 0, b1 * 0], axis=1)
  full = lambda shape: pl.BlockSpec(shape, lambda i: (0, 0))
  return pl.pallas_call(
      _hy_mlp_kernel,
      grid=(n // tl,),
      in_specs=[pl.BlockSpec((HY_FEAT_PAD, tl), lambda i: (0, i)), full((HY_HID, HY_FEAT_PAD)),
                full((HY_HID, HY_HID)), full((HY_HID, HY_HID)), full((HY_HID, 8))],
      out_specs=[pl.BlockSpec((HY_HID, tl), lambda i: (0, i))] * 2,
      out_shape=[jax.ShapeDtypeStruct((HY_HID, n), BF16)] * 2,
      compiler_params=_cparams(("parallel",)),
      name="hy_mlp",
  )(feats_t, w1t, w2.T, w3.T, pv)


def _hy_filter_kernel(hi_ref, lo_ref, wfh_ref, wfl_ref, wbh_ref, wbl_ref, tn_ref, dl_ref, o_ref):
  n = o_ref.shape[1]
  half = n // 2

  def part(wh_ref, wl_ref, sl):
    hh, hl, wh = hi_ref[:, sl], lo_ref[:, sl], wh_ref[...]
    return (jnp.dot(wh, hh, preferred_element_type=F32) + jnp.dot(wl_ref[...], hh, preferred_element_type=F32)
            + jnp.dot(wh, hl, preferred_element_type=F32))

  filt = jnp.concatenate([part(wfh_ref, wfl_ref, slice(0, half)),
                          part(wbh_ref, wbl_ref, slice(half, n))], axis=1)
  lane = lax.broadcasted_iota(jnp.int32, filt.shape, 1)
  filt = jnp.where(lane == half, 0.0, filt * jnp.exp(-tn_ref[...] * dl_ref[...]))
  o_ref[...] = filt * lax.rsqrt(jnp.sum(filt * filt, axis=-1, keepdims=True) + EPS)


def hy_filter(hdn_hi, hdn_lo, w4, tn_ext):
  n = hdn_hi.shape[1]
  c = w4.shape[1] // (2 * HY_ORDER)
  tc = 64
  w4t = w4.T.reshape(2, HY_ORDER, c, HY_HID)
  w4h = w4t.astype(BF16)
  w4l = (w4t - w4h.astype(F32)).astype(BF16)
  deltas = jnp.abs(jnp.linspace(math.log(HY_DECAY_TARGET) / HY_FAST_DECAY,
                                math.log(HY_DECAY_TARGET) / HY_SLOW_DECAY, c, dtype=F32)).reshape(c, 1)
  wspec = lambda d: pl.BlockSpec((None, None, tc, HY_HID), lambda o, i: (d, o, i, 0))
  hspec = pl.BlockSpec((HY_HID, n), lambda o, i: (0, 0))
  nct = c // tc
  return pl.pallas_call(
      _hy_filter_kernel,
      grid=(HY_ORDER, nct),
      in_specs=[hspec, hspec, wspec(0), wspec(0), wspec(1), wspec(1),
                pl.BlockSpec((1, n), lambda o, i: (0, 0)),
                pl.BlockSpec((tc, 1), lambda o, i: (i, 0))],
      out_specs=pl.BlockSpec((tc, n), lambda o, i: (o * nct + i, 0)),
      out_shape=jax.ShapeDtypeStruct((HY_ORDER * c, n), F32),
      compiler_params=_cparams(("parallel", "parallel")),
      name="hy_filter",
  )(hdn_hi, hdn_lo, w4h, w4l, w4h, w4l, tn_ext, deltas)


def _short_conv_cm(x, cw, period):
  width = x.shape[1]
  assert period & (period - 1) == 0
  t = jnp.bitwise_and(lax.broadcasted_iota(jnp.int32, x.shape, 1), period - 1)
  prev = jnp.where(t == 0, 0.0, pltpu.roll(x, 1, 1))
  nxt = jnp.where(t == period - 1, 0.0, pltpu.roll(x, width - 1, 1))
  return cw[:, 3:4] + cw[:, 0:1] * prev + cw[:, 1:2] * x + cw[:, 2:3] * nxt


def _dft_stage_a(x, fa, tc, ts):
  a = jnp.dot(fa, x.astype(BF16), preferred_element_type=F32)
  ar, ai = a[:DFT_R], a[DFT_R:]
  return jnp.concatenate([ar * tc + ai * ts, ai * tc - ar * ts], axis=1).astype(BF16)


def _hyena_lat_kernel(skip_ref, g0_ref, g1_ref, z_ref, cw0_ref, cw1_ref, cwz_ref, k0_ref, k1_ref,
                      fah_ref, faf_ref, tw_ref, gc_ref, gci_ref, fai_ref, o_ref, h0_ref, h1_ref,
                      *, n_ch):
  i = pl.program_id(0)
  b = pl.program_id(1)
  tc, ts = tw_ref[0], tw_ref[1]
  rows = HY_CG * DFT_R
  half = SEQ // DFT_R
  lane = lax.broadcasted_iota(jnp.int32, (half, DFT_R), 1)
  row = lax.broadcasted_iota(jnp.int32, (half, DFT_R), 0)
  first_lane, last_lane = lane == 0, lane == DFT_R - 1
  first_row, last_row = row == 0, row == half - 1

  def short_conv(x, cw):
    back = pltpu.roll(x, 1, 1)
    wrap = jnp.where(first_row, 0.0, pltpu.roll(back, 1, 0))
    prev = jnp.where(first_lane, wrap, back)
    fwd = pltpu.roll(x, DFT_R - 1, 1)
    wrap = jnp.where(last_row, 0.0, pltpu.roll(fwd, half - 1, 0))
    nxt = jnp.where(last_lane, wrap, fwd)
    return cw[3:4] + cw[0:1] * prev + cw[1:2] * x + cw[2:3] * nxt

  @pl.when(b == 0)
  def _():
    def spectrum(g, carry):
      base = g * HY_CG
      for k_ref, h_ref in ((k0_ref, h0_ref), (k1_ref, h1_ref)):
        ap = jnp.concatenate([_dft_stage_a(k_ref[base + c], faf_ref[...], tc, ts)
                              for c in range(HY_CG)], axis=0)
        h_ref[pl.ds(pl.multiple_of(g * rows, rows), rows), :] = jnp.dot(
            ap, gc_ref[...], preferred_element_type=F32)
      return carry
    lax.fori_loop(0, HY_CT // HY_CG, spectrum, 0)

  def group(g, carry):
    base = g * HY_CG
    zs = [short_conv(z_ref[base + c], cwz_ref[base + c]) for c in range(HY_CG)]
    for o, (h_ref, gate_ref, cw_ref) in enumerate(((h0_ref, g0_ref, cw0_ref),
                                                   (h1_ref, g1_ref, cw1_ref))):
      ap = jnp.concatenate([_dft_stage_a(zs[c], fah_ref[...], tc, ts) for c in range(HY_CG)], axis=0)
      x = jnp.dot(ap, gc_ref[...], preferred_element_type=F32)
      h = h_ref[pl.ds(pl.multiple_of(g * rows, rows), rows), :]
      xr, xi, hr, hi = x[:, :DFT_R], x[:, DFT_R:], h[:, :DFT_R], h[:, DFT_R:]
      y = jnp.concatenate([xr * hr - xi * hi, xr * hi + xi * hr], axis=1).astype(BF16)
      bb = jnp.dot(y, gci_ref[...], preferred_element_type=F32)
      nxt = []
      for c in range(HY_CG):
        bc = bb[c * DFT_R:(c + 1) * DFT_R]
        br, bi = bc[:, :DFT_R], bc[:, DFT_R:]
        st = jnp.concatenate([br * tc - bi * ts, br * ts + bi * tc], axis=0).astype(BF16)
        conv = jnp.dot(fai_ref[...], st, preferred_element_type=F32)
        sk = skip_ref[o * n_ch + i * HY_CT + base + c]
        gate = short_conv(gate_ref[base + c], cw_ref[base + c])
        nxt.append(gate * (conv + sk * zs[c]))
      zs = nxt
    for c in range(HY_CG):
      o_ref[base + c] = zs[c].astype(o_ref.dtype)
    return carry
  lax.fori_loop(0, HY_CT // HY_CG, group, 0)


def hyena_lat(hy_t, cw, kern_t, skip, consts):
  c = hy_t.shape[0] // 3
  nct = c // HY_CT
  half = SEQ // DFT_R
  u3 = hy_t.reshape(3 * c, T_PAD // DFT_R, DFT_R)
  k3 = kern_t.reshape(HY_ORDER * c, DFT_R, DFT_R)
  cwx = jnp.broadcast_to(cw[:, :, None], (3 * c, 4, DFT_R))
  uspec = lambda part: pl.BlockSpec((HY_CT, half, DFT_R), lambda i, b, sk: (part * nct + i, b, 0))
  wspec = lambda part: pl.BlockSpec((HY_CT, 4, DFT_R), lambda i, b, sk: (part * nct + i, 0, 0))
  kspec = lambda o: pl.BlockSpec((HY_CT, DFT_R, DFT_R), lambda i, b, sk: (o * nct + i, 0, 0))
  cspec = lambda a: pl.BlockSpec(a.shape, lambda i, b, sk: (0,) * a.ndim)
  cs = [consts[k] for k in ("fa_half", "fa_full", "tw", "gc", "gci", "fai")]
  out = pl.pallas_call(
      functools.partial(_hyena_lat_kernel, n_ch=c),
      grid_spec=pltpu.PrefetchScalarGridSpec(
          num_scalar_prefetch=1,
          grid=(nct, BATCH),
          in_specs=[uspec(0), uspec(1), uspec(2), wspec(0), wspec(1), wspec(2), kspec(0), kspec(1)]
          + [cspec(a) for a in cs],
          out_specs=pl.BlockSpec((HY_CT, half, DFT_R), lambda i, b, sk: (i, b, 0)),
          scratch_shapes=[pltpu.VMEM((HY_CT * DFT_R, 2 * DFT_R), F32)] * 2,
      ),
      out_shape=jax.ShapeDtypeStruct((c, N_LAT // DFT_R, DFT_R), BF16),
      compiler_params=_cparams(("parallel", "arbitrary")),
      name="hyena_lat",
  )(skip.reshape(-1), u3, u3, u3, cwx, cwx, cwx, k3, k3, *cs)
  return out.reshape(c, N_LAT)


def _hyena_ctx_kernel(g0_ref, g1_ref, z_ref, cw0_ref, cw1_ref, cwz_ref, k0_ref, k1_ref, sk_ref,
                      fwd_ref, fwdh_ref, inv_ref, o_ref):
  n = HY_N_CTX
  g0 = _short_conv_cm(g0_ref[...], cw0_ref[...], CTX_LEN)
  g1 = _short_conv_cm(g1_ref[...], cw1_ref[...], CTX_LEN)
  z = _short_conv_cm(z_ref[...], cwz_ref[...], CTX_LEN)
  sk = sk_ref[...]
  spec = [jnp.dot(k_ref[...].astype(BF16), fwd_ref[...], preferred_element_type=F32)
          for k_ref in (k0_ref, k1_ref)]
  for b in range(BATCH):
    cols = slice(b * CTX_LEN, (b + 1) * CTX_LEN)
    zb = z[:, cols]
    for o, gate in enumerate((g0, g1)):
      x = jnp.dot(zb.astype(BF16), fwdh_ref[...], preferred_element_type=F32)
      h = spec[o]
      xr, xi, hr, hi = x[:, :n], x[:, n:], h[:, :n], h[:, n:]
      y = jnp.concatenate([xr * hr - xi * hi, xr * hi + xi * hr], axis=1).astype(BF16)
      conv = jnp.dot(y, inv_ref[...], preferred_element_type=F32)
      zb = gate[:, cols] * (conv + sk[:, o:o + 1] * zb)
    o_ref[:, cols] = zb.astype(o_ref.dtype)


def hyena_ctx(hy_t, cw, kern_t, skip, consts):
  c = hy_t.shape[0] // 3
  tc = 256 if c % 256 == 0 else c
  nct = c // tc
  blk = CTX_ROW0 // N_CTX
  xspec = lambda part: pl.BlockSpec((tc, N_CTX), lambda i: (part * nct + i, blk))
  wspec = lambda part: pl.BlockSpec((tc, 4), lambda i: (part * nct + i, 0))
  kspec = lambda o: pl.BlockSpec((tc, HY_N_CTX), lambda i: (o * nct + i, 0))
  cspec = lambda a: pl.BlockSpec(a.shape, lambda i: (0,) * a.ndim)
  cs = [consts[k] for k in ("fwd", "fwd_half", "inv")]
  return pl.pallas_call(
      _hyena_ctx_kernel,
      grid=(nct,),
      in_specs=[xspec(0), xspec(1), xspec(2), wspec(0), wspec(1), wspec(2), kspec(0), kspec(1),
                pl.BlockSpec((tc, HY_ORDER), lambda i: (i, 0))] + [cspec(a) for a in cs],
      out_specs=pl.BlockSpec((tc, N_CTX), lambda i: (i, 0)),
      out_shape=jax.ShapeDtypeStruct((c, N_CTX), BF16),
      compiler_params=_cparams(("parallel",)),
      name="hyena_ctx",
  )(hy_t, hy_t, hy_t, cw, cw, cw, kern_t, kern_t, skip.T, *cs)


def hyena_cm(hy_t, conv_w, conv_b, w1, b1, w2, b2, w3, b3, w4, freq, skip):
  c = hy_t.shape[0] // 3
  cw = jnp.concatenate([conv_w.T, conv_b[:, None]], axis=1)
  outs = []
  for length, lat in ((SEQ, True), (CTX_LEN, False)):
    feats_t, tn_ext = _hy_positions(length)
    hdn_hi, hdn_lo = hy_mlp(feats_t, w1, b1, w2, b2, w3, b3, freq)
    kern_t = hy_filter(hdn_hi, hdn_lo, w4, tn_ext)
    if lat:
      outs.append(hyena_lat(hy_t, cw, kern_t, skip, _dft_consts_lat()))
    else:
      outs.append(hyena_ctx(hy_t, cw, kern_t, skip, _dft_consts_ctx()))
  outs.append(jnp.zeros((c, T_PAD - T_REAL), BF16))
  return jnp.concatenate(outs, axis=1)


def _mm_nt_kernel(x_ref, w_ref, o_ref, wt_ref):
  @pl.when(pl.program_id(1) == 0)
  def _():
    wt_ref[...] = w_ref[...].T.astype(BF16)

  o_ref[...] = lax.dot_general(wt_ref[...], x_ref[...], (((1,), (1,)), ((), ())),
                               preferred_element_type=F32).astype(o_ref.dtype)


def mm_nt(x, w, col0, n, tn=512):
  k = w.shape[0]
  return pl.pallas_call(
      _mm_nt_kernel,
      grid=(n // tn, T_PAD // TM),
      in_specs=[_xspec(k), _wspec(k, tn, col0)],
      out_specs=pl.BlockSpec((tn, TM), lambda j, i: (j, i)),
      out_shape=jax.ShapeDtypeStruct((n, T_PAD), F32),
      scratch_shapes=[pltpu.VMEM((tn, k), BF16)],
      compiler_params=_cparams(("parallel", "arbitrary")),
      name="mm_nt",
  )(x, w)


def _mm_resid2_kernel(a_ref, zt_ref, wa_ref, wz_ref, s_ref, m_ref, o_ref, wab_ref, wzb_ref):
  _cast_weight_once(wa_ref, wab_ref)
  _cast_weight_once(wz_ref, wzb_ref)
  acc = jnp.dot(a_ref[...], wab_ref[...], preferred_element_type=F32)
  acc += lax.dot_general(zt_ref[...], wzb_ref[...], (((0,), (0,)), ((), ())),
                         preferred_element_type=F32)
  o_ref[...] = s_ref[...] + m_ref[...] * acc


def mm_resid2(a, z_t, w, s, gate8, tn=512):
  ka, kz = a.shape[1], z_t.shape[0]
  n = w.shape[1]
  assert ka == kz
  row = _mod_row(TM)
  return pl.pallas_call(
      _mm_resid2_kernel,
      grid=(n // tn, T_PAD // TM),
      in_specs=[
          pl.BlockSpec((TM, ka), lambda j, i: (i, 0)),
          pl.BlockSpec((kz, TM), lambda j, i: (0, i)),
          pl.BlockSpec((ka, tn), lambda j, i: (0, j), pipeline_mode=pl.Buffered(1)),
          pl.BlockSpec((kz, tn), lambda j, i: (1, j), pipeline_mode=pl.Buffered(1)),
          pl.BlockSpec((TM, tn), lambda j, i: (i, j)),
          pl.BlockSpec((None, 1, tn), lambda j, i: (row(i), 0, j)),
      ],
      out_specs=pl.BlockSpec((TM, tn), lambda j, i: (i, j)),
      out_shape=jax.ShapeDtypeStruct((T_PAD, n), F32),
      scratch_shapes=[pltpu.VMEM((ka, tn), BF16), pltpu.VMEM((kz, tn), BF16)],
      input_output_aliases={4: 0},
      compiler_params=_cparams(("parallel", "arbitrary")),
      name="mm_resid2",
  )(a, z_t, w, w, s, gate8)


def kernel(x, c, ctx, c_ctx, ada_a, ada_b, ada_bias, norm_mix, norm_ffn, even_w_in, na_q_gain,
           na_k_gain, na_rpb, hy_conv_w, hy_conv_b, hy_w1, hy_b1, hy_w2, hy_b2, hy_w3, hy_b3,
           hy_w4, hy_freq, hy_skip, odd_w_in, odd_b_in, gm_v_gain, gm_w_s, gm_b_s, w_mix_out,
           moe_router, moe_bias, moe_w_gate, moe_w_up, moe_w_down, sh_w_gate, sh_w_up, sh_w_down):
  s = jnp.concatenate([x.reshape(N_LAT, D_MODEL), ctx.reshape(N_CTX, D_MODEL),
                       jnp.zeros((T_PAD - T_REAL, D_MODEL), F32)], axis=0)
  cond8 = jnp.concatenate([c, c_ctx[None], jnp.zeros((8 - BATCH - 1, D_MODEL), F32)], axis=0)
  mods = ada_table(cond8, ada_a, ada_b, ada_bias).reshape(DEPTH, 8, N_MOD, 1, D_MODEL)

  for layer in range(DEPTH):
    j = layer // 2
    mod = [mods[layer, :, i] for i in range(N_MOD)]
    h = norm_mod(s, norm_mix[layer], mod[0], mod[1])
    w_out = w_mix_out[layer]
    if layer % 2 == 0:
      w_in = even_w_in[j]
      gains = jnp.concatenate([jnp.tile(na_q_gain[j], NA_HEADS), jnp.tile(na_k_gain[j], NA_HEADS),
                               jnp.ones((NA_WIDTH,), F32)]).reshape(1, 3 * NA_WIDTH)
      qkv = mm_headnorm(h, w_in, gains, 3 * NA_WIDTH, 2 * NA_WIDTH)
      hy_t = mm_nt(h, w_in, 3 * NA_WIDTH, 3 * HY_WIDTH)
      a = jnp.concatenate([na_attention(qkv, na_bias(na_rpb[j])), ctx_attention(qkv),
                           jnp.zeros((T_PAD - T_REAL, NA_WIDTH), BF16)], axis=0)
      z_t = hyena_cm(hy_t, hy_conv_w[j], hy_conv_b[j], hy_w1[j], hy_b1[j], hy_w2[j], hy_b2[j],
                     hy_w3[j], hy_b3[j], hy_w4[j], hy_freq[j], hy_skip[j])
      s = mm_resid2(a, z_t, w_out, s, mod[2])
    else:
      z = mm_gelu(h, odd_w_in[j], odd_b_in[j])
      bsx = jnp.repeat(gm_b_s[j].T, HEAD_DIM, axis=1)
      mix = gmlp_gate(z, gm_v_gain[j], gm_w_s[j].astype(BF16), bsx)
      s = mm_resid(mix, w_out, s, mod[2])

    f, gates = norm_mod_router(s, norm_ffn[layer], mod[3], mod[4], moe_router[layer], moe_bias[layer])
    gates_t = gates[:, :N_EXPERTS].reshape(T_PAD, N_EXPERTS // EXP_PER_TILE, EXP_PER_TILE)
    gates_t = gates_t.transpose(1, 0, 2)
    hid_r = moe_up(f, moe_w_gate[layer], moe_w_up[layer], gates_t)
    hid_s = moe_up_shared(f, sh_w_gate[layer], sh_w_up[layer])
    s = moe_down(hid_r, hid_s, moe_w_down[layer].reshape(-1, D_MODEL).astype(BF16),
                 sh_w_down[layer].astype(BF16), s, mod[5])

  return s[:N_LAT].reshape(BATCH, SEQ, D_MODEL)
```

```python
import functools
import math

import numpy as np
import jax
import jax.numpy as jnp
from jax import lax
from jax.experimental import pallas as pl
from jax.experimental.pallas import tpu as pltpu

F32 = jnp.float32
BF16 = jnp.bfloat16

D_MODEL = 4096
BATCH = 2
SEQ = 8192
DEPTH = 4
GRID_W = 64
ROWS = SEQ // GRID_W
CTX_LEN = 256
HEAD_DIM = 128
NA_HEADS = 16
NA_WIDTH = NA_HEADS * HEAD_DIM
HY_WIDTH = D_MODEL - NA_WIDTH
WIN_H = 8
WIN_W = 16
HY_ORDER = 2
HY_SHORT = 3
HY_BANDS = 16
HY_DECAY_TARGET = 1e-2
HY_FAST_DECAY = 0.3
HY_SLOW_DECAY = 1.5
GM_WIDTH = D_MODEL
GM_GROUPS = GM_WIDTH // HEAD_DIM
CHUNK = 128
N_EXPERTS = 64
TOP_K = 8
EXPERT_DIM = 128
SHARED_DIM = TOP_K * EXPERT_DIM
ROUTED_SCALE = 2.5
N_MOD = 6
EPS = 1e-6

LANES = 128
VMEM_LIMIT = 56 * 1024 * 1024

TM = 1024
N_LAT = BATCH * SEQ
N_CTX = BATCH * CTX_LEN
T_REAL = N_LAT + N_CTX
T_PAD = -(-T_REAL // TM) * TM
CTX_ROW0 = N_LAT
TAIL_ROWS = T_PAD - N_LAT
assert TAIL_ROWS == TM
TN_NORM = 512
MOE_TN = 512
EXP_PER_TILE = MOE_TN // EXPERT_DIM
NEG = -1e30

NA_R = 4
NA_Q = NA_R * GRID_W
NA_KROWS = NA_R + WIN_H
NA_KTOK = NA_KROWS * GRID_W
NA_QBLOCKS = ROWS // NA_R
NA_HP = 2


def _mod_row(tile_rows):
  def f(i):
    return jnp.minimum((i * tile_rows) // SEQ, BATCH)
  return f


def _cparams(sem):
  return pltpu.CompilerParams(dimension_semantics=sem, vmem_limit_bytes=VMEM_LIMIT)


def _ada_kernel(cond_ref, a_ref, b_ref, bias_ref, o_ref):
  cnd = cond_ref[...]
  sc = cnd * jax.nn.sigmoid(cnd)
  t = jnp.dot(sc, a_ref[...], preferred_element_type=F32, precision=lax.Precision.HIGHEST)
  o_ref[...] = jnp.dot(t, b_ref[...], preferred_element_type=F32,
                       precision=lax.Precision.HIGHEST) + bias_ref[...]


def ada_table(cond8, ada_a, ada_b, ada_bias):
  tn = 4096
  rank = ada_a.shape[-1]
  nmd = N_MOD * D_MODEL
  return pl.pallas_call(
      _ada_kernel,
      grid=(DEPTH, nmd // tn),
      in_specs=[
          pl.BlockSpec((8, D_MODEL), lambda l, j: (0, 0)),
          pl.BlockSpec((None, D_MODEL, rank), lambda l, j: (l, 0, 0)),
          pl.BlockSpec((None, rank, tn), lambda l, j: (l, 0, j)),
          pl.BlockSpec((None, 1, tn), lambda l, j: (l, 0, j)),
      ],
      out_specs=pl.BlockSpec((None, 8, tn), lambda l, j: (l, 0, j)),
      out_shape=jax.ShapeDtypeStruct((DEPTH, 8, nmd), F32),
      compiler_params=_cparams(("arbitrary", "arbitrary")),
      name="ada_table",
  )(cond8, ada_a, ada_b, ada_bias.reshape(DEPTH, 1, nmd))


def _norm_mod(x, g, shift, scale):
  ms = jnp.mean(x * x, axis=-1, keepdims=True)
  y = x * lax.rsqrt(ms + EPS) * g
  return y * (1.0 + scale) + shift


def _norm_kernel(x_ref, g_ref, shift_ref, scale_ref, o_ref):
  o_ref[...] = _norm_mod(x_ref[...], g_ref[...], shift_ref[...], scale_ref[...]).astype(BF16)


def _norm_router_kernel(x_ref, g_ref, shift_ref, scale_ref, whi_ref, wlo_ref, rb_ref,
                        o_ref, gates_ref):
  f = _norm_mod(x_ref[...], g_ref[...], shift_ref[...], scale_ref[...])
  f_hi = f.astype(BF16)
  o_ref[...] = f_hi
  f_lo = (f - f_hi.astype(F32)).astype(BF16)
  w_hi = whi_ref[...]
  logits = (jnp.dot(f_hi, w_hi, preferred_element_type=F32)
            + jnp.dot(f_lo, w_hi, preferred_element_type=F32)
            + jnp.dot(f_hi, wlo_ref[...], preferred_element_type=F32))
  scores = jax.nn.sigmoid(logits)
  lane = lax.broadcasted_iota(jnp.int32, scores.shape, 1).astype(F32)
  work = jnp.where(lane < N_EXPERTS, scores + rb_ref[...], NEG)
  sel = jnp.zeros(scores.shape, F32)
  for _ in range(TOP_K):
    m = jnp.max(work, axis=-1, keepdims=True)
    first = jnp.min(jnp.where(work == m, lane, float(LANES)), axis=-1, keepdims=True)
    hit = lane == first
    sel = jnp.where(hit, scores, sel)
    work = jnp.where(hit, 3.0 * NEG, work)
  gates_ref[...] = sel / jnp.sum(sel, axis=-1, keepdims=True) * ROUTED_SCALE


def norm_mod(s, gain, shift8, scale8):
  n = T_PAD // TN_NORM
  row = _mod_row(TN_NORM)
  return pl.pallas_call(
      _norm_kernel,
      grid=(n,),
      in_specs=[
          pl.BlockSpec((TN_NORM, D_MODEL), lambda i: (i, 0)),
          pl.BlockSpec((1, D_MODEL), lambda i: (0, 0)),
          pl.BlockSpec((None, 1, D_MODEL), lambda i: (row(i), 0, 0)),
          pl.BlockSpec((None, 1, D_MODEL), lambda i: (row(i), 0, 0)),
      ],
      out_specs=pl.BlockSpec((TN_NORM, D_MODEL), lambda i: (i, 0)),
      out_shape=jax.ShapeDtypeStruct((T_PAD, D_MODEL), BF16),
      compiler_params=_cparams(("parallel",)),
      name="norm_mod",
  )(s, gain.reshape(1, D_MODEL), shift8, scale8)


def norm_mod_router(s, gain, shift8, scale8, w_router, r_bias):
  n = T_PAD // TN_NORM
  row = _mod_row(TN_NORM)
  w_pad = jnp.pad(w_router, ((0, 0), (0, LANES - N_EXPERTS)))
  w_hi = w_pad.astype(BF16)
  w_lo = (w_pad - w_hi.astype(F32)).astype(BF16)
  rb = jnp.pad(r_bias.astype(F32), (0, LANES - N_EXPERTS)).reshape(1, LANES)
  return pl.pallas_call(
      _norm_router_kernel,
      grid=(n,),
      in_specs=[
          pl.BlockSpec((TN_NORM, D_MODEL), lambda i: (i, 0)),
          pl.BlockSpec((1, D_MODEL), lambda i: (0, 0)),
          pl.BlockSpec((None, 1, D_MODEL), lambda i: (row(i), 0, 0)),
          pl.BlockSpec((None, 1, D_MODEL), lambda i: (row(i), 0, 0)),
          pl.BlockSpec((D_MODEL, LANES), lambda i: (0, 0)),
          pl.BlockSpec((D_MODEL, LANES), lambda i: (0, 0)),
          pl.BlockSpec((1, LANES), lambda i: (0, 0)),
      ],
      out_specs=[
          pl.BlockSpec((TN_NORM, D_MODEL), lambda i: (i, 0)),
          pl.BlockSpec((TN_NORM, LANES), lambda i: (i, 0)),
      ],
      out_shape=[
          jax.ShapeDtypeStruct((T_PAD, D_MODEL), BF16),
          jax.ShapeDtypeStruct((T_PAD, LANES), F32),
      ],
      compiler_params=_cparams(("parallel",)),
      name="norm_mod_router",
  )(s, gain.reshape(1, D_MODEL), shift8, scale8, w_hi, w_lo, rb)


def _gelu_exact(x):
  return 0.5 * x * (1.0 + lax.erf(x * (1.0 / math.sqrt(2.0))))


def _cast_weight_once(w_ref, wb_ref):
  @pl.when(pl.program_id(1) == 0)
  def _():
    wb_ref[...] = w_ref[...].astype(BF16)


def _mm_headnorm_kernel(x_ref, w_ref, g_ref, o_ref, wb_ref, *, n_norm_tiles):
  _cast_weight_once(w_ref, wb_ref)
  acc = jnp.dot(x_ref[...], wb_ref[...], preferred_element_type=F32)
  j = pl.program_id(0)

  @pl.when(j < n_norm_tiles)
  def _():
    for h in range(acc.shape[1] // HEAD_DIM):
      blk = acc[:, h * HEAD_DIM:(h + 1) * HEAD_DIM]
      ms = jnp.mean(blk * blk, axis=-1, keepdims=True)
      y = blk * lax.rsqrt(ms + EPS) * g_ref[:, h * HEAD_DIM:(h + 1) * HEAD_DIM]
      o_ref[:, h * HEAD_DIM:(h + 1) * HEAD_DIM] = y.astype(o_ref.dtype)

  @pl.when(j >= n_norm_tiles)
  def _():
    o_ref[...] = acc.astype(o_ref.dtype)


def _mm_gelu_kernel(x_ref, w_ref, b_ref, o_ref, wb_ref):
  _cast_weight_once(w_ref, wb_ref)
  acc = jnp.dot(x_ref[...], wb_ref[...], preferred_element_type=F32) + b_ref[...]
  o_ref[...] = _gelu_exact(acc).astype(o_ref.dtype)


def _mm_resid_kernel(x_ref, w_ref, s_ref, m_ref, o_ref, wb_ref):
  _cast_weight_once(w_ref, wb_ref)
  acc = jnp.dot(x_ref[...], wb_ref[...], preferred_element_type=F32)
  o_ref[...] = s_ref[...] + m_ref[...] * acc


def _moe_down_kernel(hr_ref, hs_ref, wr_ref, ws_ref, s_ref, m_ref, o_ref, acc_ref):
  k = pl.program_id(2)

  @pl.when(k == 0)
  def _():
    acc_ref[...] = jnp.zeros_like(acc_ref)

  acc_ref[...] += jnp.dot(hr_ref[...], wr_ref[...], preferred_element_type=F32)

  @pl.when(k == pl.num_programs(2) - 1)
  def _():
    acc = acc_ref[...] + jnp.dot(hs_ref[...], ws_ref[...], preferred_element_type=F32)
    o_ref[...] = s_ref[...] + m_ref[...] * acc


def _swiglu(x, wg, wu):
  g = jnp.dot(x, wg, preferred_element_type=F32)
  u = jnp.dot(x, wu, preferred_element_type=F32)
  return g * jax.nn.sigmoid(g) * u


def _moe_up_kernel(x_ref, wg_ref, wu_ref, gate_ref, o_ref, wgb_ref, wub_ref):
  @pl.when(pl.program_id(1) == 0)
  def _():
    for e in range(EXP_PER_TILE):
      sl = slice(e * EXPERT_DIM, (e + 1) * EXPERT_DIM)
      wgb_ref[:, sl] = wg_ref[e].astype(BF16)
      wub_ref[:, sl] = wu_ref[e].astype(BF16)

  hid = _swiglu(x_ref[...], wgb_ref[...], wub_ref[...])
  gate = gate_ref[...]
  for e in range(EXP_PER_TILE):
    sl = slice(e * EXPERT_DIM, (e + 1) * EXPERT_DIM)
    o_ref[:, sl] = (hid[:, sl] * gate[:, e:e + 1]).astype(o_ref.dtype)


def _moe_up_shared_kernel(x_ref, wg_ref, wu_ref, o_ref, wgb_ref, wub_ref):
  _cast_weight_once(wg_ref, wgb_ref)
  _cast_weight_once(wu_ref, wub_ref)
  o_ref[...] = _swiglu(x_ref[...], wgb_ref[...], wub_ref[...]).astype(o_ref.dtype)


def _xspec(k):
  return pl.BlockSpec((TM, k), lambda j, i: (i, 0))


def _wspec(k, tn, layer, col0=0):
  return pl.BlockSpec((None, k, tn), lambda j, i: (layer, 0, col0 // tn + j),
                      pipeline_mode=pl.Buffered(1))


def mm_headnorm(x, w, layer, gains, n, n_norm_cols, tn=1024):
  k = w.shape[1]
  return pl.pallas_call(
      functools.partial(_mm_headnorm_kernel, n_norm_tiles=n_norm_cols // tn),
      grid=(n // tn, T_PAD // TM),
      in_specs=[_xspec(k), _wspec(k, tn, layer), pl.BlockSpec((1, tn), lambda j, i: (0, j))],
      out_specs=pl.BlockSpec((TM, tn), lambda j, i: (i, j)),
      out_shape=jax.ShapeDtypeStruct((T_PAD, n), BF16),
      scratch_shapes=[pltpu.VMEM((k, tn), BF16)],
      compiler_params=_cparams(("parallel", "arbitrary")),
      name="mm_headnorm",
  )(x, w, gains)


def mm_gelu(x, w, layer, b, tn=1024):
  _, k, n = w.shape
  return pl.pallas_call(
      _mm_gelu_kernel,
      grid=(n // tn, T_PAD // TM),
      in_specs=[_xspec(k), _wspec(k, tn, layer), pl.BlockSpec((1, tn), lambda j, i: (0, j))],
      out_specs=pl.BlockSpec((TM, tn), lambda j, i: (i, j)),
      out_shape=jax.ShapeDtypeStruct((T_PAD, n), BF16),
      scratch_shapes=[pltpu.VMEM((k, tn), BF16)],
      compiler_params=_cparams(("parallel", "arbitrary")),
      name="mm_gelu",
  )(x, w, b.reshape(1, n))


def mm_resid(x, w, layer, s, gate8, tn=512):
  _, k, n = w.shape
  row = _mod_row(TM)
  return pl.pallas_call(
      _mm_resid_kernel,
      grid=(n // tn, T_PAD // TM),
      in_specs=[_xspec(k), _wspec(k, tn, layer),
                pl.BlockSpec((TM, tn), lambda j, i: (i, j)),
                pl.BlockSpec((None, 1, tn), lambda j, i: (row(i), 0, j))],
      out_specs=pl.BlockSpec((TM, tn), lambda j, i: (i, j)),
      out_shape=jax.ShapeDtypeStruct((T_PAD, n), F32),
      scratch_shapes=[pltpu.VMEM((k, tn), BF16)],
      input_output_aliases={2: 0},
      compiler_params=_cparams(("parallel", "arbitrary")),
      name="mm_resid",
  )(x, w, s, gate8)


def moe_down(hid_r, hid_s, wd_r, wd_s, layer, s, gate8, out_rows=T_PAD, tn=1024, tk=2048):
  _, kr, n = wd_r.shape
  ks = wd_s.shape[1]
  row = _mod_row(TM)
  return pl.pallas_call(
      _moe_down_kernel,
      grid=(n // tn, out_rows // TM, kr // tk),
      in_specs=[
          pl.BlockSpec((TM, tk), lambda j, i, kk: (i, kk)),
          pl.BlockSpec((TM, ks), lambda j, i, kk: (i, 0)),
          pl.BlockSpec((None, tk, tn), lambda j, i, kk: (layer, kk, j)),
          pl.BlockSpec((None, ks, tn), lambda j, i, kk: (layer, 0, j)),
          pl.BlockSpec((TM, tn), lambda j, i, kk: (i, j)),
          pl.BlockSpec((None, 1, tn), lambda j, i, kk: (row(i), 0, j)),
      ],
      out_specs=pl.BlockSpec((TM, tn), lambda j, i, kk: (i, j)),
      out_shape=jax.ShapeDtypeStruct((out_rows, n), F32),
      scratch_shapes=[pltpu.VMEM((TM, tn), F32)],
      input_output_aliases={4: 0} if out_rows == T_PAD else {},
      compiler_params=_cparams(("parallel", "parallel", "arbitrary")),
      name="moe_down",
  )(hid_r, hid_s, wd_r, wd_s, s, gate8)


def moe_up(x, w_gate, w_up, layer, gates_t):
  _, n_exp, k, f = w_gate.shape
  tn = MOE_TN
  wspec = pl.BlockSpec((None, EXP_PER_TILE, k, f), lambda j, i: (layer, j, 0, 0),
                       pipeline_mode=pl.Buffered(1))
  return pl.pallas_call(
      _moe_up_kernel,
      grid=(n_exp // EXP_PER_TILE, T_PAD // TM),
      in_specs=[_xspec(k), wspec, wspec,
                pl.BlockSpec((None, TM, EXP_PER_TILE), lambda j, i: (j, i, 0))],
      out_specs=pl.BlockSpec((TM, tn), lambda j, i: (i, j)),
      out_shape=jax.ShapeDtypeStruct((T_PAD, n_exp * f), BF16),
      scratch_shapes=[pltpu.VMEM((k, tn), BF16)] * 2,
      compiler_params=_cparams(("parallel", "arbitrary")),
      name="moe_up",
  )(x, w_gate, w_up, gates_t)


def moe_up_shared(x, wg, wu, layer):
  _, k, n = wg.shape
  tn = MOE_TN
  return pl.pallas_call(
      _moe_up_shared_kernel,
      grid=(n // tn, T_PAD // TM),
      in_specs=[_xspec(k), _wspec(k, tn, layer), _wspec(k, tn, layer)],
      out_specs=pl.BlockSpec((TM, tn), lambda j, i: (i, j)),
      out_shape=jax.ShapeDtypeStruct((T_PAD, n), BF16),
      scratch_shapes=[pltpu.VMEM((k, tn), BF16)] * 2,
      compiler_params=_cparams(("parallel", "arbitrary")),
      name="moe_up_shared",
  )(x, wg, wu)


def _na_kernel(q_ref, k_ref, v_ref, kc_ref, vc_ref, bias_ref, o_ref):
  qb = pl.program_id(2)
  kb = jnp.clip(qb * NA_R - WIN_H // 2, 0, ROWS - NA_KROWS)
  start = pl.multiple_of(kb * GRID_W, GRID_W)
  kind = jnp.where(qb == 0, 0, jnp.where(qb == NA_QBLOCKS - 1, 2, 1))
  scale = HEAD_DIM ** -0.5
  nt = (((1,), (1,)), ((), ()))
  for hh in range(NA_HP):
    cols = slice(hh * HEAD_DIM, (hh + 1) * HEAD_DIM)
    q = q_ref[:, cols]
    kw = k_ref[pl.ds(start, NA_KTOK), cols]
    vw = v_ref[pl.ds(start, NA_KTOK), cols]
    s = lax.dot_general(q, kw, nt, preferred_element_type=F32) * scale + bias_ref[kind, hh]
    sc = lax.dot_general(q, kc_ref[:, cols], nt, preferred_element_type=F32) * scale
    m = jnp.maximum(jnp.max(s, axis=-1, keepdims=True), jnp.max(sc, axis=-1, keepdims=True))
    p = jnp.exp(s - m)
    pc = jnp.exp(sc - m)
    den = jnp.sum(p, axis=-1, keepdims=True) + jnp.sum(pc, axis=-1, keepdims=True)
    o = (jnp.dot(p.astype(BF16), vw, preferred_element_type=F32)
         + jnp.dot(pc.astype(BF16), vc_ref[:, cols], preferred_element_type=F32))
    o_ref[:, cols] = (o / den).astype(o_ref.dtype)


def _na_bias_selectors():
  rows = np.zeros((3, NA_R, NA_KROWS, 2 * WIN_H - 1), np.float32)
  for kind, qb in enumerate((0, 1, NA_QBLOCKS - 1)):
    r0 = qb * NA_R
    kb = int(np.clip(r0 - WIN_H // 2, 0, ROWS - NA_KROWS))
    for i in range(NA_R):
      r = r0 + i
      rs = int(np.clip(r - WIN_H // 2, 0, ROWS - WIN_H))
      for j in range(NA_KROWS):
        if rs <= kb + j < rs + WIN_H:
          rows[kind, i, j, kb + j - r + WIN_H - 1] = 1.0
  cols = np.zeros((2 * WIN_W - 1, GRID_W, GRID_W), np.float32)
  for c in range(GRID_W):
    cs = int(np.clip(c - WIN_W // 2, 0, GRID_W - WIN_W))
    for d in range(cs, cs + WIN_W):
      cols[d - c + WIN_W - 1, c, d] = 1.0
  return rows, cols


def na_bias(rpb):
  rows, cols = _na_bias_selectors()
  hp = lax.Precision.HIGHEST
  colband = jnp.einsum('hrw,wcd->hrcd', rpb.astype(F32), cols, precision=hp)
  band = jnp.einsum('tijr,hrcd->thicjd', rows, colband, precision=hp)
  valid = np.einsum('tijr,wcd->ticjd', rows, cols) > 0.5
  band = jnp.where(valid[:, None], band, NEG)
  return band.reshape(3, NA_HEADS, NA_Q, NA_KTOK)


def na_attention(qkv, bias):
  hq = NA_HEADS // NA_HP
  width = NA_HP * HEAD_DIM
  lat_blocks = SEQ // NA_Q
  ctx_blk0 = CTX_ROW0 // CTX_LEN
  return pl.pallas_call(
      _na_kernel,
      grid=(BATCH, hq, NA_QBLOCKS),
      in_specs=[
          pl.BlockSpec((NA_Q, width), lambda b, h, i: (b * lat_blocks + i, h)),
          pl.BlockSpec((SEQ, width), lambda b, h, i: (b, hq + h)),
          pl.BlockSpec((SEQ, width), lambda b, h, i: (b, 2 * hq + h)),
          pl.BlockSpec((CTX_LEN, width), lambda b, h, i: (ctx_blk0 + b, hq + h)),
          pl.BlockSpec((CTX_LEN, width), lambda b, h, i: (ctx_blk0 + b, 2 * hq + h)),
          pl.BlockSpec((3, NA_HP, NA_Q, NA_KTOK), lambda b, h, i: (0, h, 0, 0)),
      ],
      out_specs=pl.BlockSpec((NA_Q, width), lambda b, h, i: (b * lat_blocks + i, h)),
      out_shape=jax.ShapeDtypeStruct((N_LAT, NA_WIDTH), BF16),
      compiler_params=_cparams(("parallel", "parallel", "arbitrary")),
      name="na_attention",
  )(qkv, qkv, qkv, qkv, qkv, bias)


def _ctx_attn_kernel(q_ref, k_ref, v_ref, o_ref):
  @pl.when(pl.program_id(0) < BATCH)
  def _():
    scale = HEAD_DIM ** -0.5
    s = lax.dot_general(q_ref[...], k_ref[...], (((1,), (1,)), ((), ())),
                        preferred_element_type=F32) * scale
    m = jnp.max(s, axis=-1, keepdims=True)
    p = jnp.exp(s - m)
    den = jnp.sum(p, axis=-1, keepdims=True)
    o = jnp.dot(p.astype(BF16), v_ref[...], preferred_element_type=F32)
    o_ref[...] = (o / den).astype(o_ref.dtype)

  @pl.when(pl.program_id(0) >= BATCH)
  def _():
    o_ref[...] = jnp.zeros_like(o_ref)


def ctx_attention(qkv):
  hq = NA_HEADS
  ctx_blk0 = CTX_ROW0 // CTX_LEN
  blk = lambda b: ctx_blk0 + jnp.minimum(b, BATCH - 1)
  return pl.pallas_call(
      _ctx_attn_kernel,
      grid=(TAIL_ROWS // CTX_LEN, NA_HEADS),
      in_specs=[
          pl.BlockSpec((CTX_LEN, HEAD_DIM), lambda b, h: (blk(b), h)),
          pl.BlockSpec((CTX_LEN, HEAD_DIM), lambda b, h: (blk(b), hq + h)),
          pl.BlockSpec((CTX_LEN, HEAD_DIM), lambda b, h: (blk(b), 2 * hq + h)),
      ],
      out_specs=pl.BlockSpec((CTX_LEN, HEAD_DIM), lambda b, h: (b, h)),
      out_shape=jax.ShapeDtypeStruct((TAIL_ROWS, NA_WIDTH), BF16),
      compiler_params=_cparams(("parallel", "parallel")),
      name="ctx_attention",
  )(qkv, qkv, qkv)


def _gmlp_gate_kernel(u_ref, v_ref, vg_ref, ws_ref, bs_ref, o_ref):
  v = v_ref[...].astype(F32)
  ms = jnp.mean(v * v, axis=-1, keepdims=True)
  vn = (v * lax.rsqrt(ms + EPS) * vg_ref[...]).astype(BF16)
  for c in range(v.shape[0] // CHUNK):
    rows = slice(c * CHUNK, (c + 1) * CHUNK)
    for g in range(GM_GROUPS):
      cols = slice(g * HEAD_DIM, (g + 1) * HEAD_DIM)
      sp = jnp.dot(ws_ref[g], vn[rows, cols], preferred_element_type=F32) + bs_ref[:, cols]
      o_ref[rows, cols] = (u_ref[rows, cols].astype(F32) * sp).astype(o_ref.dtype)


def gmlp_gate(z, v_gain, w_s, bsx):
  tg = 256
  ncol = GM_WIDTH // GM_WIDTH
  return pl.pallas_call(
      _gmlp_gate_kernel,
      grid=(T_PAD // tg,),
      in_specs=[
          pl.BlockSpec((tg, GM_WIDTH), lambda i: (i, 0)),
          pl.BlockSpec((tg, GM_WIDTH), lambda i: (i, ncol)),
          pl.BlockSpec((1, GM_WIDTH), lambda i: (0, 0)),
          pl.BlockSpec((GM_GROUPS, CHUNK, CHUNK), lambda i: (0, 0, 0)),
          pl.BlockSpec((CHUNK, GM_WIDTH), lambda i: (0, 0)),
      ],
      out_specs=pl.BlockSpec((tg, GM_WIDTH), lambda i: (i, 0)),
      out_shape=jax.ShapeDtypeStruct((T_PAD, GM_WIDTH), BF16),
      compiler_params=_cparams(("parallel",)),
      name="gmlp_gate",
  )(z, z, v_gain.reshape(1, GM_WIDTH), w_s, bsx)


HY_FEAT = 1 + 2 * HY_BANDS
HY_FEAT_PAD = 40
HY_HID = 64
DFT_R = 128
HY_N_LAT = 2 * SEQ
HY_N_CTX = 2 * CTX_LEN
HY_CT = 32
HY_CG = 8
assert HY_N_LAT == DFT_R * DFT_R


def _dft_consts_lat():
  n = np.arange(DFT_R)
  ang = 2.0 * np.pi * np.outer(n, n) / DFT_R
  c, s = np.cos(ang), np.sin(ang)
  tw = 2.0 * np.pi * np.outer(n, n) / HY_N_LAT
  half = SEQ // DFT_R
  fa = np.concatenate([c, -s], axis=0)
  gc = np.block([[c, -s], [s, c]])
  gci = np.block([[c, s], [-s, c]])
  fai = np.concatenate([c[:half], -s[:half]], axis=1) / HY_N_LAT
  return dict(
      fa_full=jnp.asarray(fa, BF16), fa_half=jnp.asarray(fa[:, :half], BF16),
      tw=jnp.asarray(np.stack([np.cos(tw), np.sin(tw)]), F32),
      gc=jnp.asarray(gc, BF16), gci=jnp.asarray(gci, BF16), fai=jnp.asarray(fai, BF16))


def _dft_consts_ctx():
  n = np.arange(HY_N_CTX)
  ang = 2.0 * np.pi * np.outer(n, n) / HY_N_CTX
  c, s = np.cos(ang), np.sin(ang)
  fwd = np.concatenate([c, -s], axis=1)
  inv = np.concatenate([c[:, :CTX_LEN], -s[:, :CTX_LEN]], axis=0) / HY_N_CTX
  return dict(fwd=jnp.asarray(fwd, BF16), fwd_half=jnp.asarray(fwd[:CTX_LEN], BF16),
              inv=jnp.asarray(inv, BF16))


def _hy_positions(length):
  t_norm = jnp.linspace(0.0, 1.0, length, dtype=F32)[:, None]
  pos = jnp.arange(length, dtype=F32)[:, None]
  bands = jnp.linspace(1e-4, HY_BANDS - 1, HY_BANDS, dtype=F32)[None, :]
  ang = (2.0 * math.pi / length) * pos * bands
  feats = jnp.concatenate([t_norm, jnp.cos(ang), -jnp.sin(ang)], axis=-1)
  ext = lambda a: jnp.concatenate([a, a[:1], a[:0:-1]], axis=0)
  feats_t = jnp.pad(ext(feats).T, ((0, HY_FEAT_PAD - HY_FEAT), (0, 0)))
  return feats_t, ext(t_norm).T


def _hy_mlp_kernel(ft_ref, w1_ref, w2_ref, w3_ref, pv_ref, hi_ref, lo_ref):
  hp = lax.Precision.HIGHEST
  pv = pv_ref[...]
  h = jnp.sin(pv[:, 3:4] * (jnp.dot(w1_ref[...], ft_ref[...], precision=hp,
                                    preferred_element_type=F32) + pv[:, 0:1]))
  h = jnp.sin(pv[:, 4:5] * (jnp.dot(w2_ref[...], h, precision=hp,
                                    preferred_element_type=F32) + pv[:, 1:2]))
  h = jnp.sin(pv[:, 5:6] * (jnp.dot(w3_ref[...], h, precision=hp,
                                    preferred_element_type=F32) + pv[:, 2:3]))
  hi = h.astype(BF16)
  hi_ref[...] = hi
  lo_ref[...] = (h - hi.astype(F32)).astype(BF16)


def hy_mlp(feats_t, w1, b1, w2, b2, w3, b3, freq):
  n = feats_t.shape[1]
  tl = min(n, 2048)
  w1t = jnp.pad(w1.T, ((0, 0), (0, HY_FEAT_PAD - HY_FEAT)))
  pv = jnp.stack([b1, b2, b3, freq[0], freq[1], freq[2], b1 * 0, b1 * 0], axis=1)
  full = lambda shape: pl.BlockSpec(shape, lambda i: (0, 0))
  return pl.pallas_call(
      _hy_mlp_kernel,
      grid=(n // tl,),
      in_specs=[pl.BlockSpec((HY_FEAT_PAD, tl), lambda i: (0, i)), full((HY_HID, HY_FEAT_PAD)),
                full((HY_HID, HY_HID)), full((HY_HID, HY_HID)), full((HY_HID, 8))],
      out_specs=[pl.BlockSpec((HY_HID, tl), lambda i: (0, i))] * 2,
      out_shape=[jax.ShapeDtypeStruct((HY_HID, n), BF16)] * 2,
      compiler_params=_cparams(("parallel",)),
      name="hy_mlp",
  )(feats_t, w1t, w2.T, w3.T, pv)


def _hy_filter_kernel(hi_ref, lo_ref, wfh_ref, wfl_ref, wbh_ref, wbl_ref, tn_ref, dl_ref, o_ref):
  n = o_ref.shape[1]
  half = n // 2

  def part(wh_ref, wl_ref, sl):
    hh, hl, wh = hi_ref[:, sl], lo_ref[:, sl], wh_ref[...]
    return (jnp.dot(wh, hh, preferred_element_type=F32) + jnp.dot(wl_ref[...], hh, preferred_element_type=F32)
            + jnp.dot(wh, hl, preferred_element_type=F32))

  filt = jnp.concatenate([part(wfh_ref, wfl_ref, slice(0, half)),
                          part(wbh_ref, wbl_ref, slice(half, n))], axis=1)
  lane = lax.broadcasted_iota(jnp.int32, filt.shape, 1)
  filt = jnp.where(lane == half, 0.0, filt * jnp.exp(-tn_ref[...] * dl_ref[...]))
  o_ref[...] = filt * lax.rsqrt(jnp.sum(filt * filt, axis=-1, keepdims=True) + EPS)


def hy_filter(hdn_hi, hdn_lo, w4, tn_ext):
  n = hdn_hi.shape[1]
  c = w4.shape[1] // (2 * HY_ORDER)
  tc = 64
  w4t = w4.T.reshape(2, HY_ORDER, c, HY_HID)
  w4h = w4t.astype(BF16)
  w4l = (w4t - w4h.astype(F32)).astype(BF16)
  deltas = jnp.abs(jnp.linspace(math.log(HY_DECAY_TARGET) / HY_FAST_DECAY,
                                math.log(HY_DECAY_TARGET) / HY_SLOW_DECAY, c, dtype=F32)).reshape(c, 1)
  wspec = lambda d: pl.BlockSpec((None, None, tc, HY_HID), lambda o, i: (d, o, i, 0))
  hspec = pl.BlockSpec((HY_HID, n), lambda o, i: (0, 0))
  nct = c // tc
  return pl.pallas_call(
      _hy_filter_kernel,
      grid=(HY_ORDER, nct),
      in_specs=[hspec, hspec, wspec(0), wspec(0), wspec(1), wspec(1),
                pl.BlockSpec((1, n), lambda o, i: (0, 0)),
                pl.BlockSpec((tc, 1), lambda o, i: (i, 0))],
      out_specs=pl.BlockSpec((tc, n), lambda o, i: (o * nct + i, 0)),
      out_shape=jax.ShapeDtypeStruct((HY_ORDER * c, n), F32),
      compiler_params=_cparams(("parallel", "parallel")),
      name="hy_filter",
  )(hdn_hi, hdn_lo, w4h, w4l, w4h, w4l, tn_ext, deltas)


def _short_conv_cm(x, cw, period):
  width = x.shape[1]
  assert period & (period - 1) == 0
  t = jnp.bitwise_and(lax.broadcasted_iota(jnp.int32, x.shape, 1), period - 1)
  prev = jnp.where(t == 0, 0.0, pltpu.roll(x, 1, 1))
  nxt = jnp.where(t == period - 1, 0.0, pltpu.roll(x, width - 1, 1))
  return cw[:, 3:4] + cw[:, 0:1] * prev + cw[:, 1:2] * x + cw[:, 2:3] * nxt


def _dft_stage_a(x, fa, tc, ts):
  a = jnp.dot(fa, x.astype(BF16), preferred_element_type=F32)
  ar, ai = a[:DFT_R], a[DFT_R:]
  return jnp.concatenate([ar * tc + ai * ts, ai * tc - ar * ts], axis=1).astype(BF16)


def _hyena_lat_kernel(skip_ref, g0_ref, g1_ref, z_ref, cw0_ref, cw1_ref, cwz_ref, k0_ref, k1_ref,
                      fah_ref, faf_ref, tw_ref, gc_ref, gci_ref, fai_ref, o_ref, h0_ref, h1_ref,
                      *, n_ch):
  i = pl.program_id(0)
  b = pl.program_id(1)
  tc, ts = tw_ref[0], tw_ref[1]
  rows = HY_CG * DFT_R
  half = SEQ // DFT_R
  lane = lax.broadcasted_iota(jnp.int32, (half, DFT_R), 1)
  row = lax.broadcasted_iota(jnp.int32, (half, DFT_R), 0)
  first_lane, last_lane = lane == 0, lane == DFT_R - 1
  first_row, last_row = row == 0, row == half - 1

  def short_conv(x, cw):
    back = pltpu.roll(x, 1, 1)
    wrap = jnp.where(first_row, 0.0, pltpu.roll(back, 1, 0))
    prev = jnp.where(first_lane, wrap, back)
    fwd = pltpu.roll(x, DFT_R - 1, 1)
    wrap = jnp.where(last_row, 0.0, pltpu.roll(fwd, half - 1, 0))
    nxt = jnp.where(last_lane, wrap, fwd)
    return cw[3:4] + cw[0:1] * prev + cw[1:2] * x + cw[2:3] * nxt

  @pl.when(b == 0)
  def _():
    def spectrum(g, carry):
      base = g * HY_CG
      for k_ref, h_ref in ((k0_ref, h0_ref), (k1_ref, h1_ref)):
        ap = jnp.concatenate([_dft_stage_a(k_ref[base + c], faf_ref[...], tc, ts)
                              for c in range(HY_CG)], axis=0)
        h_ref[pl.ds(pl.multiple_of(g * rows, rows), rows), :] = jnp.dot(
            ap, gc_ref[...], preferred_element_type=F32)
      return carry
    lax.fori_loop(0, HY_CT // HY_CG, spectrum, 0)

  def group(g, carry):
    base = g * HY_CG
    zs = [short_conv(z_ref[base + c], cwz_ref[base + c]) for c in range(HY_CG)]
    for o, (h_ref, gate_ref, cw_ref) in enumerate(((h0_ref, g0_ref, cw0_ref),
                                                   (h1_ref, g1_ref, cw1_ref))):
      ap = jnp.concatenate([_dft_stage_a(zs[c], fah_ref[...], tc, ts) for c in range(HY_CG)], axis=0)
      x = jnp.dot(ap, gc_ref[...], preferred_element_type=F32)
      h = h_ref[pl.ds(pl.multiple_of(g * rows, rows), rows), :]
      xr, xi, hr, hi = x[:, :DFT_R], x[:, DFT_R:], h[:, :DFT_R], h[:, DFT_R:]
      y = jnp.concatenate([xr * hr - xi * hi, xr * hi + xi * hr], axis=1).astype(BF16)
      bb = jnp.dot(y, gci_ref[...], preferred_element_type=F32)
      nxt = []
      for c in range(HY_CG):
        bc = bb[c * DFT_R:(c + 1) * DFT_R]
        br, bi = bc[:, :DFT_R], bc[:, DFT_R:]
        st = jnp.concatenate([br * tc - bi * ts, br * ts + bi * tc], axis=0).astype(BF16)
        conv = jnp.dot(fai_ref[...], st, preferred_element_type=F32)
        sk = skip_ref[o * n_ch + i * HY_CT + base + c]
        gate = short_conv(gate_ref[base + c], cw_ref[base + c])
        nxt.append(gate * (conv + sk * zs[c]))
      zs = nxt
    for c in range(HY_CG):
      o_ref[base + c] = zs[c].astype(o_ref.dtype)
    return carry
  lax.fori_loop(0, HY_CT // HY_CG, group, 0)


def hyena_lat(hy_t, cw, kern_t, skip, consts):
  c = hy_t.shape[0] // 3
  nct = c // HY_CT
  half = SEQ // DFT_R
  u3 = hy_t.reshape(3 * c, T_PAD // DFT_R, DFT_R)
  k3 = kern_t.reshape(HY_ORDER * c, DFT_R, DFT_R)
  cwx = jnp.broadcast_to(cw[:, :, None], (3 * c, 4, DFT_R))
  uspec = lambda part: pl.BlockSpec((HY_CT, half, DFT_R), lambda i, b, sk: (part * nct + i, b, 0))
  wspec = lambda part: pl.BlockSpec((HY_CT, 4, DFT_R), lambda i, b, sk: (part * nct + i, 0, 0))
  kspec = lambda o: pl.BlockSpec((HY_CT, DFT_R, DFT_R), lambda i, b, sk: (o * nct + i, 0, 0))
  cspec = lambda a: pl.BlockSpec(a.shape, lambda i, b, sk: (0,) * a.ndim)
  cs = [consts[k] for k in ("fa_half", "fa_full", "tw", "gc", "gci", "fai")]
  out = pl.pallas_call(
      functools.partial(_hyena_lat_kernel, n_ch=c),
      grid_spec=pltpu.PrefetchScalarGridSpec(
          num_scalar_prefetch=1,
          grid=(nct, BATCH),
          in_specs=[uspec(0), uspec(1), uspec(2), wspec(0), wspec(1), wspec(2), kspec(0), kspec(1)]
          + [cspec(a) for a in cs],
          out_specs=pl.BlockSpec((HY_CT, half, DFT_R), lambda i, b, sk: (i, b, 0)),
          scratch_shapes=[pltpu.VMEM((HY_CT * DFT_R, 2 * DFT_R), F32)] * 2,
      ),
      out_shape=jax.ShapeDtypeStruct((c, N_LAT // DFT_R, DFT_R), BF16),
      compiler_params=_cparams(("parallel", "arbitrary")),
      name="hyena_lat",
  )(skip.reshape(-1), u3, u3, u3, cwx, cwx, cwx, k3, k3, *cs)
  return out.reshape(c, N_LAT)


def _hyena_ctx_kernel(g0_ref, g1_ref, z_ref, cw0_ref, cw1_ref, cwz_ref, k0_ref, k1_ref, sk_ref,
                      fwd_ref, fwdh_ref, inv_ref, o_ref):
  n = HY_N_CTX
  g0 = _short_conv_cm(g0_ref[...], cw0_ref[...], CTX_LEN)
  g1 = _short_conv_cm(g1_ref[...], cw1_ref[...], CTX_LEN)
  z = _short_conv_cm(z_ref[...], cwz_ref[...], CTX_LEN)
  sk = sk_ref[...]
  spec = [jnp.dot(k_ref[...].astype(BF16), fwd_ref[...], preferred_element_type=F32)
          for k_ref in (k0_ref, k1_ref)]
  for b in range(BATCH):
    cols = slice(b * CTX_LEN, (b + 1) * CTX_LEN)
    zb = z[:, cols]
    for o, gate in enumerate((g0, g1)):
      x = jnp.dot(zb.astype(BF16), fwdh_ref[...], preferred_element_type=F32)
      h = spec[o]
      xr, xi, hr, hi = x[:, :n], x[:, n:], h[:, :n], h[:, n:]
      y = jnp.concatenate([xr * hr - xi * hi, xr * hi + xi * hr], axis=1).astype(BF16)
      conv = jnp.dot(y, inv_ref[...], preferred_element_type=F32)
      zb = gate[:, cols] * (conv + sk[:, o:o + 1] * zb)
    o_ref[:, cols] = zb.astype(o_ref.dtype)
  o_ref[:, N_CTX:] = jnp.zeros((o_ref.shape[0], TAIL_ROWS - N_CTX), o_ref.dtype)


def hyena_ctx(hy_t, cw, kern_t, skip, consts):
  c = hy_t.shape[0] // 3
  tc = 256 if c % 256 == 0 else c
  nct = c // tc
  blk = CTX_ROW0 // N_CTX
  xspec = lambda part: pl.BlockSpec((tc, N_CTX), lambda i: (part * nct + i, blk))
  wspec = lambda part: pl.BlockSpec((tc, 4), lambda i: (part * nct + i, 0))
  kspec = lambda o: pl.BlockSpec((tc, HY_N_CTX), lambda i: (o * nct + i, 0))
  cspec = lambda a: pl.BlockSpec(a.shape, lambda i: (0,) * a.ndim)
  cs = [consts[k] for k in ("fwd", "fwd_half", "inv")]
  return pl.pallas_call(
      _hyena_ctx_kernel,
      grid=(nct,),
      in_specs=[xspec(0), xspec(1), xspec(2), wspec(0), wspec(1), wspec(2), kspec(0), kspec(1),
                pl.BlockSpec((tc, HY_ORDER), lambda i: (i, 0))] + [cspec(a) for a in cs],
      out_specs=pl.BlockSpec((tc, TAIL_ROWS), lambda i: (i, 0)),
      out_shape=jax.ShapeDtypeStruct((c, TAIL_ROWS), BF16),
      compiler_params=_cparams(("parallel",)),
      name="hyena_ctx",
  )(hy_t, hy_t, hy_t, cw, cw, cw, kern_t, kern_t, skip.T, *cs)


def hyena_cm(hy_t, conv_w, conv_b, w1, b1, w2, b2, w3, b3, w4, freq, skip):
  cw = jnp.concatenate([conv_w.T, conv_b[:, None]], axis=1)
  outs = []
  for length, lat in ((SEQ, True), (CTX_LEN, False)):
    feats_t, tn_ext = _hy_positions(length)
    hdn_hi, hdn_lo = hy_mlp(feats_t, w1, b1, w2, b2, w3, b3, freq)
    kern_t = hy_filter(hdn_hi, hdn_lo, w4, tn_ext)
    if lat:
      outs.append(hyena_lat(hy_t, cw, kern_t, skip, _dft_consts_lat()))
    else:
      outs.append(hyena_ctx(hy_t, cw, kern_t, skip, _dft_consts_ctx()))
  return tuple(outs)


def _mm_nt_kernel(x_ref, w_ref, o_ref, wt_ref):
  @pl.when(pl.program_id(1) == 0)
  def _():
    wt_ref[...] = w_ref[...].T.astype(BF16)

  o_ref[...] = lax.dot_general(wt_ref[...], x_ref[...], (((1,), (1,)), ((), ())),
                               preferred_element_type=F32).astype(o_ref.dtype)


def mm_nt(x, w, layer, col0, n, tn=512):
  k = w.shape[1]
  return pl.pallas_call(
      _mm_nt_kernel,
      grid=(n // tn, T_PAD // TM),
      in_specs=[_xspec(k), _wspec(k, tn, layer, col0)],
      out_specs=pl.BlockSpec((tn, TM), lambda j, i: (j, i)),
      out_shape=jax.ShapeDtypeStruct((n, T_PAD), F32),
      scratch_shapes=[pltpu.VMEM((tn, k), BF16)],
      compiler_params=_cparams(("parallel", "arbitrary")),
      name="mm_nt",
  )(x, w)


def _mm_resid2_kernel(al_ref, ac_ref, zl_ref, zc_ref, wa_ref, wz_ref, s_ref, m_ref, o_ref,
                      wab_ref, wzb_ref, *, n_lat_tiles):
  _cast_weight_once(wa_ref, wab_ref)
  _cast_weight_once(wz_ref, wzb_ref)
  i = pl.program_id(1)

  def mix(a_ref, zt_ref):
    acc = jnp.dot(a_ref[...], wab_ref[...], preferred_element_type=F32)
    acc += lax.dot_general(zt_ref[...], wzb_ref[...], (((0,), (0,)), ((), ())),
                           preferred_element_type=F32)
    o_ref[...] = s_ref[...] + m_ref[...] * acc

  @pl.when(i < n_lat_tiles)
  def _():
    mix(al_ref, zl_ref)

  @pl.when(i >= n_lat_tiles)
  def _():
    mix(ac_ref, zc_ref)


def mm_resid2(a_lat, a_tail, z_lat_t, z_tail_t, w, layer, s, gate8, tn=512):
  ka, kz = a_lat.shape[1], z_lat_t.shape[0]
  n = w.shape[2]
  n_lat_tiles = N_LAT // TM
  assert ka == kz and a_tail.shape[0] == TM and z_tail_t.shape[1] == TM
  row = _mod_row(TM)
  lat = lambda i: jnp.minimum(i, n_lat_tiles - 1)
  once = pl.Buffered(1)
  return pl.pallas_call(
      functools.partial(_mm_resid2_kernel, n_lat_tiles=n_lat_tiles),
      grid=(n // tn, T_PAD // TM),
      in_specs=[
          pl.BlockSpec((TM, ka), lambda j, i: (lat(i), 0)),
          pl.BlockSpec((TM, ka), lambda j, i: (0, 0), pipeline_mode=once),
          pl.BlockSpec((kz, TM), lambda j, i: (0, lat(i))),
          pl.BlockSpec((kz, TM), lambda j, i: (0, 0), pipeline_mode=once),
          pl.BlockSpec((None, ka, tn), lambda j, i: (layer, 0, j), pipeline_mode=once),
          pl.BlockSpec((None, kz, tn), lambda j, i: (layer, 1, j), pipeline_mode=once),
          pl.BlockSpec((TM, tn), lambda j, i: (i, j)),
          pl.BlockSpec((None, 1, tn), lambda j, i: (row(i), 0, j)),
      ],
      out_specs=pl.BlockSpec((TM, tn), lambda j, i: (i, j)),
      out_shape=jax.ShapeDtypeStruct((T_PAD, n), F32),
      scratch_shapes=[pltpu.VMEM((ka, tn), BF16), pltpu.VMEM((kz, tn), BF16)],
      input_output_aliases={6: 0},
      compiler_params=_cparams(("parallel", "arbitrary")),
      name="mm_resid2",
  )(a_lat, a_tail, z_lat_t, z_tail_t, w, w, s, gate8)


def kernel(x, c, ctx, c_ctx, ada_a, ada_b, ada_bias, norm_mix, norm_ffn, even_w_in, na_q_gain,
           na_k_gain, na_rpb, hy_conv_w, hy_conv_b, hy_w1, hy_b1, hy_w2, hy_b2, hy_w3, hy_b3,
           hy_w4, hy_freq, hy_skip, odd_w_in, odd_b_in, gm_v_gain, gm_w_s, gm_b_s, w_mix_out,
           moe_router, moe_bias, moe_w_gate, moe_w_up, moe_w_down, sh_w_gate, sh_w_up, sh_w_down):
  s = jnp.concatenate([x.reshape(N_LAT, D_MODEL), ctx.reshape(N_CTX, D_MODEL),
                       jnp.zeros((T_PAD - T_REAL, D_MODEL), F32)], axis=0)
  cond8 = jnp.concatenate([c, c_ctx[None], jnp.zeros((8 - BATCH - 1, D_MODEL), F32)], axis=0)
  mods = ada_table(cond8, ada_a, ada_b, ada_bias).reshape(DEPTH, 8, N_MOD, 1, D_MODEL)
  wd_r = moe_w_down.reshape(DEPTH, N_EXPERTS * EXPERT_DIM, D_MODEL).astype(BF16)
  wd_s = sh_w_down.astype(BF16)

  for layer in range(DEPTH):
    j = layer // 2
    mod = [mods[layer, :, i] for i in range(N_MOD)]
    h = norm_mod(s, norm_mix[layer], mod[0], mod[1])
    if layer % 2 == 0:
      gains = jnp.concatenate([jnp.tile(na_q_gain[j], NA_HEADS), jnp.tile(na_k_gain[j], NA_HEADS),
                               jnp.ones((NA_WIDTH,), F32)]).reshape(1, 3 * NA_WIDTH)
      qkv = mm_headnorm(h, even_w_in, j, gains, 3 * NA_WIDTH, 2 * NA_WIDTH)
      hy_t = mm_nt(h, even_w_in, j, 3 * NA_WIDTH, 3 * HY_WIDTH)
      a_lat = na_attention(qkv, na_bias(na_rpb[j]))
      a_tail = ctx_attention(qkv)
      z_lat_t, z_tail_t = hyena_cm(hy_t, hy_conv_w[j], hy_conv_b[j], hy_w1[j], hy_b1[j], hy_w2[j],
                                   hy_b2[j], hy_w3[j], hy_b3[j], hy_w4[j], hy_freq[j], hy_skip[j])
      s = mm_resid2(a_lat, a_tail, z_lat_t, z_tail_t, w_mix_out, layer, s, mod[2])
    else:
      z = mm_gelu(h, odd_w_in, j, odd_b_in[j])
      bsx = jnp.repeat(gm_b_s[j].T, HEAD_DIM, axis=1)
      mix = gmlp_gate(z, gm_v_gain[j], gm_w_s[j].astype(BF16), bsx)
      s = mm_resid(mix, w_mix_out, layer, s, mod[2])

    f, gates = norm_mod_router(s, norm_ffn[layer], mod[3], mod[4], moe_router[layer], moe_bias[layer])
    gates_t = gates[:, :N_EXPERTS].reshape(T_PAD, N_EXPERTS // EXP_PER_TILE, EXP_PER_TILE)
    gates_t = gates_t.transpose(1, 0, 2)
    hid_r = moe_up(f, moe_w_gate, moe_w_up, layer, gates_t)
    hid_s = moe_up_shared(f, sh_w_gate, sh_w_up, layer)
    out_rows = N_LAT if layer == DEPTH - 1 else T_PAD
    s = moe_down(hid_r, hid_s, wd_r, wd_s, layer, s, mod[5], out_rows)

  return s.reshape(BATCH, SEQ, D_MODEL)
```

```python
import functools
import math

import numpy as np
import jax
import jax.numpy as jnp
from jax import lax
from jax.experimental import pallas as pl
from jax.experimental.pallas import tpu as pltpu

F32 = jnp.float32
BF16 = jnp.bfloat16

D_MODEL = 4096
BATCH = 2
SEQ = 8192
DEPTH = 4
GRID_W = 64
ROWS = SEQ // GRID_W
CTX_LEN = 256
HEAD_DIM = 128
NA_HEADS = 16
NA_WIDTH = NA_HEADS * HEAD_DIM
HY_WIDTH = D_MODEL - NA_WIDTH
WIN_H = 8
WIN_W = 16
HY_ORDER = 2
HY_SHORT = 3
HY_BANDS = 16
HY_DECAY_TARGET = 1e-2
HY_FAST_DECAY = 0.3
HY_SLOW_DECAY = 1.5
GM_WIDTH = D_MODEL
GM_GROUPS = GM_WIDTH // HEAD_DIM
CHUNK = 128
N_EXPERTS = 64
TOP_K = 8
EXPERT_DIM = 128
SHARED_DIM = TOP_K * EXPERT_DIM
ROUTED_SCALE = 2.5
N_MOD = 6
EPS = 1e-6

LANES = 128
VMEM_LIMIT = 56 * 1024 * 1024

TM = 1024
N_LAT = BATCH * SEQ
N_CTX = BATCH * CTX_LEN
T_REAL = N_LAT + N_CTX
T_PAD = -(-T_REAL // TM) * TM
CTX_ROW0 = N_LAT
TAIL_ROWS = T_PAD - N_LAT
assert TAIL_ROWS == TM
TN_NORM = 512
MOE_TN = 512
EXP_PER_TILE = MOE_TN // EXPERT_DIM
NEG = -1e30

NA_R = 4
NA_Q = NA_R * GRID_W
NA_KROWS = NA_R + WIN_H
NA_KTOK = NA_KROWS * GRID_W
NA_QBLOCKS = ROWS // NA_R
NA_HP = 2


def _mod_row(tile_rows):
  def f(i):
    return jnp.minimum((i * tile_rows) // SEQ, BATCH)
  return f


def _cparams(sem):
  return pltpu.CompilerParams(dimension_semantics=sem, vmem_limit_bytes=VMEM_LIMIT)


def _ada_kernel(cond_ref, a_ref, b_ref, bias_ref, o_ref):
  cnd = cond_ref[...]
  sc = cnd * jax.nn.sigmoid(cnd)
  t = jnp.dot(sc, a_ref[...], preferred_element_type=F32, precision=lax.Precision.HIGHEST)
  o_ref[...] = jnp.dot(t, b_ref[...], preferred_element_type=F32,
                       precision=lax.Precision.HIGHEST) + bias_ref[...]


def ada_table(cond8, ada_a, ada_b, ada_bias):
  tn = 4096
  rank = ada_a.shape[-1]
  nmd = N_MOD * D_MODEL
  return pl.pallas_call(
      _ada_kernel,
      grid=(DEPTH, nmd // tn),
      in_specs=[
          pl.BlockSpec((8, D_MODEL), lambda l, j: (0, 0)),
          pl.BlockSpec((None, D_MODEL, rank), lambda l, j: (l, 0, 0)),
          pl.BlockSpec((None, rank, tn), lambda l, j: (l, 0, j)),
          pl.BlockSpec((None, 1, tn), lambda l, j: (l, 0, j)),
      ],
      out_specs=pl.BlockSpec((None, 8, tn), lambda l, j: (l, 0, j)),
      out_shape=jax.ShapeDtypeStruct((DEPTH, 8, nmd), F32),
      compiler_params=_cparams(("arbitrary", "arbitrary")),
      name="ada_table",
  )(cond8, ada_a, ada_b, ada_bias.reshape(DEPTH, 1, nmd))


def _norm_mod(x, g, shift, scale):
  ms = jnp.mean(x * x, axis=-1, keepdims=True)
  y = x * lax.rsqrt(ms + EPS) * g
  return y * (1.0 + scale) + shift


def _norm_kernel(x_ref, g_ref, shift_ref, scale_ref, o_ref):
  o_ref[...] = _norm_mod(x_ref[...], g_ref[...], shift_ref[...], scale_ref[...]).astype(BF16)


def _norm_router_kernel(x_ref, g_ref, shift_ref, scale_ref, whi_ref, wlo_ref, rb_ref,
                        o_ref, gates_ref):
  f = _norm_mod(x_ref[...], g_ref[...], shift_ref[...], scale_ref[...])
  f_hi = f.astype(BF16)
  o_ref[...] = f_hi
  f_lo = (f - f_hi.astype(F32)).astype(BF16)
  w_hi = whi_ref[...]
  logits = (jnp.dot(f_hi, w_hi, preferred_element_type=F32)
            + jnp.dot(f_lo, w_hi, preferred_element_type=F32)
            + jnp.dot(f_hi, wlo_ref[...], preferred_element_type=F32))
  scores = jax.nn.sigmoid(logits)
  lane = lax.broadcasted_iota(jnp.int32, scores.shape, 1).astype(F32)
  work = jnp.where(lane < N_EXPERTS, scores + rb_ref[...], NEG)
  sel = jnp.zeros(scores.shape, F32)
  for _ in range(TOP_K):
    m = jnp.max(work, axis=-1, keepdims=True)
    first = jnp.min(jnp.where(work == m, lane, float(LANES)), axis=-1, keepdims=True)
    hit = lane == first
    sel = jnp.where(hit, scores, sel)
    work = jnp.where(hit, 3.0 * NEG, work)
  gates_ref[...] = sel / jnp.sum(sel, axis=-1, keepdims=True) * ROUTED_SCALE


def norm_mod(s, gain, shift8, scale8, rows=T_PAD):
  n = rows // TN_NORM
  row = _mod_row(TN_NORM)
  return pl.pallas_call(
      _norm_kernel,
      grid=(n,),
      in_specs=[
          pl.BlockSpec((TN_NORM, D_MODEL), lambda i: (i, 0)),
          pl.BlockSpec((1, D_MODEL), lambda i: (0, 0)),
          pl.BlockSpec((None, 1, D_MODEL), lambda i: (row(i), 0, 0)),
          pl.BlockSpec((None, 1, D_MODEL), lambda i: (row(i), 0, 0)),
      ],
      out_specs=pl.BlockSpec((TN_NORM, D_MODEL), lambda i: (i, 0)),
      out_shape=jax.ShapeDtypeStruct((rows, D_MODEL), BF16),
      compiler_params=_cparams(("parallel",)),
      name="norm_mod",
  )(s, gain.reshape(1, D_MODEL), shift8, scale8)


def norm_mod_router(s, gain, shift8, scale8, w_router, r_bias, rows=T_PAD):
  n = rows // TN_NORM
  row = _mod_row(TN_NORM)
  w_pad = jnp.pad(w_router, ((0, 0), (0, LANES - N_EXPERTS)))
  w_hi = w_pad.astype(BF16)
  w_lo = (w_pad - w_hi.astype(F32)).astype(BF16)
  rb = jnp.pad(r_bias.astype(F32), (0, LANES - N_EXPERTS)).reshape(1, LANES)
  return pl.pallas_call(
      _norm_router_kernel,
      grid=(n,),
      in_specs=[
          pl.BlockSpec((TN_NORM, D_MODEL), lambda i: (i, 0)),
          pl.BlockSpec((1, D_MODEL), lambda i: (0, 0)),
          pl.BlockSpec((None, 1, D_MODEL), lambda i: (row(i), 0, 0)),
          pl.BlockSpec((None, 1, D_MODEL), lambda i: (row(i), 0, 0)),
          pl.BlockSpec((D_MODEL, LANES), lambda i: (0, 0)),
          pl.BlockSpec((D_MODEL, LANES), lambda i: (0, 0)),
          pl.BlockSpec((1, LANES), lambda i: (0, 0)),
      ],
      out_specs=[
          pl.BlockSpec((TN_NORM, D_MODEL), lambda i: (i, 0)),
          pl.BlockSpec((TN_NORM, LANES), lambda i: (i, 0)),
      ],
      out_shape=[
          jax.ShapeDtypeStruct((rows, D_MODEL), BF16),
          jax.ShapeDtypeStruct((rows, LANES), F32),
      ],
      compiler_params=_cparams(("parallel",)),
      name="norm_mod_router",
  )(s, gain.reshape(1, D_MODEL), shift8, scale8, w_hi, w_lo, rb)


def _gelu_exact(x):
  return 0.5 * x * (1.0 + lax.erf(x * (1.0 / math.sqrt(2.0))))


MXU_N = 256


def _col_chunks(width):
  return [slice(c, c + MXU_N) for c in range(0, width, MXU_N)]


def _mm_headnorm_kernel(x_ref, w_ref, g_ref, o_ref, *, n_norm_tiles):
  j = pl.program_id(0)
  x = x_ref[...]

  @pl.when(j < n_norm_tiles)
  def _():
    for cs in _col_chunks(o_ref.shape[1]):
      acc = jnp.dot(x, w_ref[:, cs], preferred_element_type=F32)
      for h in range(MXU_N // HEAD_DIM):
        blk = acc[:, h * HEAD_DIM:(h + 1) * HEAD_DIM]
        hs = slice(cs.start + h * HEAD_DIM, cs.start + (h + 1) * HEAD_DIM)
        ms = jnp.mean(blk * blk, axis=-1, keepdims=True)
        o_ref[:, hs] = (blk * lax.rsqrt(ms + EPS) * g_ref[:, hs]).astype(o_ref.dtype)

  @pl.when(j >= n_norm_tiles)
  def _():
    o_ref[...] = jnp.dot(x, w_ref[...], preferred_element_type=F32).astype(o_ref.dtype)


def _mm_gelu_kernel(x_ref, w_ref, b_ref, o_ref):
  x = x_ref[...]
  for cs in _col_chunks(o_ref.shape[1]):
    acc = jnp.dot(x, w_ref[:, cs], preferred_element_type=F32) + b_ref[:, cs]
    o_ref[:, cs] = _gelu_exact(acc).astype(o_ref.dtype)


def _mm_resid_kernel(x_ref, w_ref, s_ref, m_ref, o_ref):
  acc = jnp.dot(x_ref[...], w_ref[...], preferred_element_type=F32)
  o_ref[...] = s_ref[...] + m_ref[...] * acc


def _moe_down_kernel(hr_ref, hs_ref, wr_ref, ws_ref, s_ref, m_ref, o_ref, acc_ref):
  k = pl.program_id(2)

  @pl.when(k == 0)
  def _():
    acc_ref[...] = jnp.zeros_like(acc_ref)

  acc_ref[...] += jnp.dot(hr_ref[...], wr_ref[...], preferred_element_type=F32)

  @pl.when(k == pl.num_programs(2) - 1)
  def _():
    acc = acc_ref[...] + jnp.dot(hs_ref[...], ws_ref[...], preferred_element_type=F32)
    o_ref[...] = s_ref[...] + m_ref[...] * acc


def _swiglu(x, wg, wu):
  g = jnp.dot(x, wg, preferred_element_type=F32)
  u = jnp.dot(x, wu, preferred_element_type=F32)
  return g * jax.nn.sigmoid(g) * u


def _moe_up_kernel(x_ref, wg_ref, wu_ref, gate_ref, o_ref, wgb_ref, wub_ref):
  @pl.when(pl.program_id(1) == 0)
  def _():
    for e in range(EXP_PER_TILE):
      sl = slice(e * EXPERT_DIM, (e + 1) * EXPERT_DIM)
      wgb_ref[:, sl] = wg_ref[e].astype(BF16)
      wub_ref[:, sl] = wu_ref[e].astype(BF16)

  x = x_ref[...]
  gate = gate_ref[...]
  for cs in _col_chunks(o_ref.shape[1]):
    hid = _swiglu(x, wgb_ref[:, cs], wub_ref[:, cs])
    for e in range(MXU_N // EXPERT_DIM):
      col = cs.start + e * EXPERT_DIM
      g = gate[:, col // EXPERT_DIM:col // EXPERT_DIM + 1]
      o_ref[:, col:col + EXPERT_DIM] = (hid[:, e * EXPERT_DIM:(e + 1) * EXPERT_DIM] * g).astype(o_ref.dtype)


def _moe_up_shared_kernel(x_ref, wg_ref, wu_ref, o_ref):
  x = x_ref[...]
  for cs in _col_chunks(o_ref.shape[1]):
    o_ref[:, cs] = _swiglu(x, wg_ref[:, cs], wu_ref[:, cs]).astype(o_ref.dtype)


def _xspec(k):
  return pl.BlockSpec((TM, k), lambda j, i: (i, 0))


def _wspec(k, tn, layer, col0=0):
  return pl.BlockSpec((None, k, tn), lambda j, i: (layer, 0, col0 // tn + j))


def mm_headnorm(x, w, layer, gains, n, n_norm_cols, tn=1024):
  k = w.shape[1]
  return pl.pallas_call(
      functools.partial(_mm_headnorm_kernel, n_norm_tiles=n_norm_cols // tn),
      grid=(n // tn, T_PAD // TM),
      in_specs=[_xspec(k), _wspec(k, tn, layer), pl.BlockSpec((1, tn), lambda j, i: (0, j))],
      out_specs=pl.BlockSpec((TM, tn), lambda j, i: (i, j)),
      out_shape=jax.ShapeDtypeStruct((T_PAD, n), BF16),
      compiler_params=_cparams(("parallel", "parallel")),
      name="mm_headnorm",
  )(x, w, gains)


def mm_gelu(x, w, layer, b, rows=T_PAD, tn=1024):
  _, k, n = w.shape
  return pl.pallas_call(
      _mm_gelu_kernel,
      grid=(n // tn, rows // TM),
      in_specs=[_xspec(k), _wspec(k, tn, layer), pl.BlockSpec((1, tn), lambda j, i: (0, j))],
      out_specs=pl.BlockSpec((TM, tn), lambda j, i: (i, j)),
      out_shape=jax.ShapeDtypeStruct((rows, n), BF16),
      compiler_params=_cparams(("parallel", "parallel")),
      name="mm_gelu",
  )(x, w, b.reshape(1, n))


def mm_resid(x, w, layer, s, gate8, rows=T_PAD, tn=512):
  _, k, n = w.shape
  row = _mod_row(TM)
  return pl.pallas_call(
      _mm_resid_kernel,
      grid=(n // tn, rows // TM),
      in_specs=[_xspec(k), _wspec(k, tn, layer),
                pl.BlockSpec((TM, tn), lambda j, i: (i, j)),
                pl.BlockSpec((None, 1, tn), lambda j, i: (row(i), 0, j))],
      out_specs=pl.BlockSpec((TM, tn), lambda j, i: (i, j)),
      out_shape=jax.ShapeDtypeStruct(s.shape, F32),
      input_output_aliases={2: 0},
      compiler_params=_cparams(("parallel", "parallel")),
      name="mm_resid",
  )(x, w, s, gate8)


def moe_down(hid_r, hid_s, wd_r, wd_s, layer, s, gate8, out_rows=T_PAD, tn=1024, tk=2048):
  _, kr, n = wd_r.shape
  ks = wd_s.shape[1]
  row = _mod_row(TM)
  return pl.pallas_call(
      _moe_down_kernel,
      grid=(n // tn, out_rows // TM, kr // tk),
      in_specs=[
          pl.BlockSpec((TM, tk), lambda j, i, kk: (i, kk)),
          pl.BlockSpec((TM, ks), lambda j, i, kk: (i, 0)),
          pl.BlockSpec((None, tk, tn), lambda j, i, kk: (layer, kk, j)),
          pl.BlockSpec((None, ks, tn), lambda j, i, kk: (layer, 0, j)),
          pl.BlockSpec((TM, tn), lambda j, i, kk: (i, j)),
          pl.BlockSpec((None, 1, tn), lambda j, i, kk: (row(i), 0, j)),
      ],
      out_specs=pl.BlockSpec((TM, tn), lambda j, i, kk: (i, j)),
      out_shape=jax.ShapeDtypeStruct((out_rows, n), F32),
      scratch_shapes=[pltpu.VMEM((TM, tn), F32)],
      input_output_aliases={4: 0} if out_rows == T_PAD else {},
      compiler_params=_cparams(("parallel", "parallel", "arbitrary")),
      name="moe_down",
  )(hid_r, hid_s, wd_r, wd_s, s, gate8)


def moe_up(x, w_gate, w_up, layer, gates_t, rows=T_PAD):
  _, n_exp, k, f = w_gate.shape
  tn = MOE_TN
  wspec = pl.BlockSpec((None, EXP_PER_TILE, k, f), lambda j, i: (layer, j, 0, 0),
                       pipeline_mode=pl.Buffered(1))
  return pl.pallas_call(
      _moe_up_kernel,
      grid=(n_exp // EXP_PER_TILE, rows // TM),
      in_specs=[_xspec(k), wspec, wspec,
                pl.BlockSpec((None, TM, EXP_PER_TILE), lambda j, i: (j, i, 0))],
      out_specs=pl.BlockSpec((TM, tn), lambda j, i: (i, j)),
      out_shape=jax.ShapeDtypeStruct((rows, n_exp * f), BF16),
      scratch_shapes=[pltpu.VMEM((k, tn), BF16)] * 2,
      compiler_params=_cparams(("parallel", "arbitrary")),
      name="moe_up",
  )(x, w_gate, w_up, gates_t)


def moe_up_shared(x, wg, wu, layer, rows=T_PAD):
  _, k, n = wg.shape
  tn = MOE_TN
  return pl.pallas_call(
      _moe_up_shared_kernel,
      grid=(n // tn, rows // TM),
      in_specs=[_xspec(k), _wspec(k, tn, layer), _wspec(k, tn, layer)],
      out_specs=pl.BlockSpec((TM, tn), lambda j, i: (i, j)),
      out_shape=jax.ShapeDtypeStruct((rows, n), BF16),
      compiler_params=_cparams(("parallel", "parallel")),
      name="moe_up_shared",
  )(x, wg, wu)


def _na_kernel(q_ref, k_ref, v_ref, kc_ref, vc_ref, bias_ref, o_ref):
  qb = pl.program_id(2)
  kb = jnp.clip(qb * NA_R - WIN_H // 2, 0, ROWS - NA_KROWS)
  start = pl.multiple_of(kb * GRID_W, GRID_W)
  kind = jnp.where(qb == 0, 0, jnp.where(qb == NA_QBLOCKS - 1, 2, 1))
  scale = HEAD_DIM ** -0.5
  nt = (((1,), (1,)), ((), ()))
  for hh in range(NA_HP):
    cols = slice(hh * HEAD_DIM, (hh + 1) * HEAD_DIM)
    q = q_ref[:, cols]
    kw = k_ref[pl.ds(start, NA_KTOK), cols]
    vw = v_ref[pl.ds(start, NA_KTOK), cols]
    s = lax.dot_general(q, kw, nt, preferred_element_type=F32) * scale + bias_ref[kind, hh]
    sc = lax.dot_general(q, kc_ref[:, cols], nt, preferred_element_type=F32) * scale
    m = jnp.maximum(jnp.max(s, axis=-1, keepdims=True), jnp.max(sc, axis=-1, keepdims=True))
    p = jnp.exp(s - m)
    pc = jnp.exp(sc - m)
    den = jnp.sum(p, axis=-1, keepdims=True) + jnp.sum(pc, axis=-1, keepdims=True)
    o = (jnp.dot(p.astype(BF16), vw, preferred_element_type=F32)
         + jnp.dot(pc.astype(BF16), vc_ref[:, cols], preferred_element_type=F32))
    o_ref[:, cols] = (o / den).astype(o_ref.dtype)


def _na_bias_selectors():
  rows = np.zeros((3, NA_R, NA_KROWS, 2 * WIN_H - 1), np.float32)
  for kind, qb in enumerate((0, 1, NA_QBLOCKS - 1)):
    r0 = qb * NA_R
    kb = int(np.clip(r0 - WIN_H // 2, 0, ROWS - NA_KROWS))
    for i in range(NA_R):
      r = r0 + i
      rs = int(np.clip(r - WIN_H // 2, 0, ROWS - WIN_H))
      for j in range(NA_KROWS):
        if rs <= kb + j < rs + WIN_H:
          rows[kind, i, j, kb + j - r + WIN_H - 1] = 1.0
  cols = np.zeros((2 * WIN_W - 1, GRID_W, GRID_W), np.float32)
  for c in range(GRID_W):
    cs = int(np.clip(c - WIN_W // 2, 0, GRID_W - WIN_W))
    for d in range(cs, cs + WIN_W):
      cols[d - c + WIN_W - 1, c, d] = 1.0
  return rows, cols


def na_bias(rpb):
  rows, cols = _na_bias_selectors()
  hp = lax.Precision.HIGHEST
  colband = jnp.einsum('hrw,wcd->hrcd', rpb.astype(F32), cols, precision=hp)
  band = jnp.einsum('tijr,hrcd->thicjd', rows, colband, precision=hp)
  valid = np.einsum('tijr,wcd->ticjd', rows, cols) > 0.5
  band = jnp.where(valid[:, None], band, NEG)
  return band.reshape(3, NA_HEADS, NA_Q, NA_KTOK)


def na_attention(qkv, bias):
  hq = NA_HEADS // NA_HP
  width = NA_HP * HEAD_DIM
  lat_blocks = SEQ // NA_Q
  ctx_blk0 = CTX_ROW0 // CTX_LEN
  return pl.pallas_call(
      _na_kernel,
      grid=(BATCH, hq, NA_QBLOCKS),
      in_specs=[
          pl.BlockSpec((NA_Q, width), lambda b, h, i: (b * lat_blocks + i, h)),
          pl.BlockSpec((SEQ, width), lambda b, h, i: (b, hq + h)),
          pl.BlockSpec((SEQ, width), lambda b, h, i: (b, 2 * hq + h)),
          pl.BlockSpec((CTX_LEN, width), lambda b, h, i: (ctx_blk0 + b, hq + h)),
          pl.BlockSpec((CTX_LEN, width), lambda b, h, i: (ctx_blk0 + b, 2 * hq + h)),
          pl.BlockSpec((3, NA_HP, NA_Q, NA_KTOK), lambda b, h, i: (0, h, 0, 0)),
      ],
      out_specs=pl.BlockSpec((NA_Q, width), lambda b, h, i: (b * lat_blocks + i, h)),
      out_shape=jax.ShapeDtypeStruct((N_LAT, NA_WIDTH), BF16),
      compiler_params=_cparams(("parallel", "parallel", "arbitrary")),
      name="na_attention",
  )(qkv, qkv, qkv, qkv, qkv, bias)


def _ctx_attn_kernel(q_ref, k_ref, v_ref, o_ref):
  @pl.when(pl.program_id(0) < BATCH)
  def _():
    scale = HEAD_DIM ** -0.5
    s = lax.dot_general(q_ref[...], k_ref[...], (((1,), (1,)), ((), ())),
                        preferred_element_type=F32) * scale
    m = jnp.max(s, axis=-1, keepdims=True)
    p = jnp.exp(s - m)
    den = jnp.sum(p, axis=-1, keepdims=True)
    o = jnp.dot(p.astype(BF16), v_ref[...], preferred_element_type=F32)
    o_ref[...] = (o / den).astype(o_ref.dtype)

  @pl.when(pl.program_id(0) >= BATCH)
  def _():
    o_ref[...] = jnp.zeros_like(o_ref)


def ctx_attention(qkv):
  hq = NA_HEADS
  ctx_blk0 = CTX_ROW0 // CTX_LEN
  blk = lambda b: ctx_blk0 + jnp.minimum(b, BATCH - 1)
  return pl.pallas_call(
      _ctx_attn_kernel,
      grid=(TAIL_ROWS // CTX_LEN, NA_HEADS),
      in_specs=[
          pl.BlockSpec((CTX_LEN, HEAD_DIM), lambda b, h: (blk(b), h)),
          pl.BlockSpec((CTX_LEN, HEAD_DIM), lambda b, h: (blk(b), hq + h)),
          pl.BlockSpec((CTX_LEN, HEAD_DIM), lambda b, h: (blk(b), 2 * hq + h)),
      ],
      out_specs=pl.BlockSpec((CTX_LEN, HEAD_DIM), lambda b, h: (b, h)),
      out_shape=jax.ShapeDtypeStruct((TAIL_ROWS, NA_WIDTH), BF16),
      compiler_params=_cparams(("parallel", "parallel")),
      name="ctx_attention",
  )(qkv, qkv, qkv)


def _gmlp_gate_kernel(u_ref, v_ref, vg_ref, ws_ref, bs_ref, o_ref):
  v = v_ref[...].astype(F32)
  ms = jnp.mean(v * v, axis=-1, keepdims=True)
  vn = (v * lax.rsqrt(ms + EPS) * vg_ref[...]).astype(BF16)
  for c in range(v.shape[0] // CHUNK):
    rows = slice(c * CHUNK, (c + 1) * CHUNK)
    for g in range(GM_GROUPS):
      cols = slice(g * HEAD_DIM, (g + 1) * HEAD_DIM)
      sp = jnp.dot(ws_ref[g], vn[rows, cols], preferred_element_type=F32) + bs_ref[:, cols]
      o_ref[rows, cols] = (u_ref[rows, cols].astype(F32) * sp).astype(o_ref.dtype)


def gmlp_gate(z, v_gain, w_s, bsx):
  tg = 256
  rows = z.shape[0]
  return pl.pallas_call(
      _gmlp_gate_kernel,
      grid=(rows // tg,),
      in_specs=[
          pl.BlockSpec((tg, GM_WIDTH), lambda i: (i, 0)),
          pl.BlockSpec((tg, GM_WIDTH), lambda i: (i, 1)),
          pl.BlockSpec((1, GM_WIDTH), lambda i: (0, 0)),
          pl.BlockSpec((GM_GROUPS, CHUNK, CHUNK), lambda i: (0, 0, 0)),
          pl.BlockSpec((CHUNK, GM_WIDTH), lambda i: (0, 0)),
      ],
      out_specs=pl.BlockSpec((tg, GM_WIDTH), lambda i: (i, 0)),
      out_shape=jax.ShapeDtypeStruct((rows, GM_WIDTH), BF16),
      compiler_params=_cparams(("parallel",)),
      name="gmlp_gate",
  )(z, z, v_gain.reshape(1, GM_WIDTH), w_s, bsx)


HY_FEAT = 1 + 2 * HY_BANDS
HY_FEAT_PAD = 40
HY_HID = 64
DFT_R = 128
HY_N_LAT = 2 * SEQ
HY_N_CTX = 2 * CTX_LEN
HY_CT = 32
HY_CG = 8
assert HY_N_LAT == DFT_R * DFT_R


def _dft_consts_lat():
  n = np.arange(DFT_R)
  ang = 2.0 * np.pi * np.outer(n, n) / DFT_R
  c, s = np.cos(ang), np.sin(ang)
  tw = 2.0 * np.pi * np.outer(n, n) / HY_N_LAT
  half = SEQ // DFT_R
  fa = np.concatenate([c, -s], axis=0)
  gc = np.block([[c, -s], [s, c]])
  gci = np.block([[c, s], [-s, c]])
  fai = np.concatenate([c[:half], -s[:half]], axis=1) / HY_N_LAT
  return dict(
      fa_full=jnp.asarray(fa, BF16), fa_half=jnp.asarray(fa[:, :half], BF16),
      tw=jnp.asarray(np.stack([np.cos(tw), np.sin(tw)]), F32),
      gc=jnp.asarray(gc, BF16), gci=jnp.asarray(gci, BF16), fai=jnp.asarray(fai, BF16))


def _dft_consts_ctx():
  n = np.arange(HY_N_CTX)
  ang = 2.0 * np.pi * np.outer(n, n) / HY_N_CTX
  c, s = np.cos(ang), np.sin(ang)
  fwd = np.concatenate([c, -s], axis=1)
  inv = np.concatenate([c[:, :CTX_LEN], -s[:, :CTX_LEN]], axis=0) / HY_N_CTX
  return dict(fwd=jnp.asarray(fwd, BF16), fwd_half=jnp.asarray(fwd[:CTX_LEN], BF16),
              inv=jnp.asarray(inv, BF16))


def _hy_positions(length):
  t_norm = jnp.linspace(0.0, 1.0, length, dtype=F32)[:, None]
  pos = jnp.arange(length, dtype=F32)[:, None]
  bands = jnp.linspace(1e-4, HY_BANDS - 1, HY_BANDS, dtype=F32)[None, :]
  ang = (2.0 * math.pi / length) * pos * bands
  feats = jnp.concatenate([t_norm, jnp.cos(ang), -jnp.sin(ang)], axis=-1)
  ext = lambda a: jnp.concatenate([a, a[:1], a[:0:-1]], axis=0)
  feats_t = jnp.pad(ext(feats).T, ((0, HY_FEAT_PAD - HY_FEAT), (0, 0)))
  return feats_t, ext(t_norm).T


def _hy_mlp_kernel(ft_ref, w1_ref, w2_ref, w3_ref, pv_ref, hi_ref, lo_ref):
  hp = lax.Precision.HIGHEST
  pv = pv_ref[...]
  h = jnp.sin(pv[:, 3:4] * (jnp.dot(w1_ref[...], ft_ref[...], precision=hp,
                                    preferred_element_type=F32) + pv[:, 0:1]))
  h = jnp.sin(pv[:, 4:5] * (jnp.dot(w2_ref[...], h, precision=hp,
                                    preferred_element_type=F32) + pv[:, 1:2]))
  h = jnp.sin(pv[:, 5:6] * (jnp.dot(w3_ref[...], h, precision=hp,
                                    preferred_element_type=F32) + pv[:, 2:3]))
  hi = h.astype(BF16)
  hi_ref[...] = hi
  lo_ref[...] = (h - hi.astype(F32)).astype(BF16)


def hy_mlp(feats_t, w1, b1, w2, b2, w3, b3, freq):
  n = feats_t.shape[1]
  tl = min(n, 2048)
  w1t = jnp.pad(w1.T, ((0, 0), (0, HY_FEAT_PAD - HY_FEAT)))
  pv = jnp.stack([b1, b2, b3, freq[0], freq[1], freq[2], b1 * 0, b1 * 0], axis=1)
  full = lambda shape: pl.BlockSpec(shape, lambda i: (0, 0))
  return pl.pallas_call(
      _hy_mlp_kernel,
      grid=(n // tl,),
      in_specs=[pl.BlockSpec((HY_FEAT_PAD, tl), lambda i: (0, i)), full((HY_HID, HY_FEAT_PAD)),
                full((HY_HID, HY_HID)), full((HY_HID, HY_HID)), full((HY_HID, 8))],
      out_specs=[pl.BlockSpec((HY_HID, tl), lambda i: (0, i))] * 2,
      out_shape=[jax.ShapeDtypeStruct((HY_HID, n), BF16)] * 2,
      compiler_params=_cparams(("parallel",)),
      name="hy_mlp",
  )(feats_t, w1t, w2.T, w3.T, pv)


def _hy_filter_kernel(hi_ref, lo_ref, wfh_ref, wfl_ref, wbh_ref, wbl_ref, tn_ref, dl_ref, o_ref):
  n = hi_ref.shape[1]
  half = n // 2

  def part(wh_ref, wl_ref, sl):
    hh, hl, wh = hi_ref[:, sl], lo_ref[:, sl], wh_ref[...]
    return (jnp.dot(wh, hh, preferred_element_type=F32) + jnp.dot(wl_ref[...], hh, preferred_element_type=F32)
            + jnp.dot(wh, hl, preferred_element_type=F32))

  filt = jnp.concatenate([part(wfh_ref, wfl_ref, slice(0, half)),
                          part(wbh_ref, wbl_ref, slice(half, n))], axis=1)
  lane = lax.broadcasted_iota(jnp.int32, filt.shape, 1)
  filt = jnp.where(lane == half, 0.0, filt * jnp.exp(-tn_ref[...] * dl_ref[...]))
  kern = filt * lax.rsqrt(jnp.sum(filt * filt, axis=-1, keepdims=True) + EPS)
  if len(o_ref.shape) == 2:
    o_ref[...] = kern
  else:
    for a in range(o_ref.shape[1]):
      o_ref[:, a, :] = kern[:, a * DFT_R:(a + 1) * DFT_R]


def hy_filter(hdn_hi, hdn_lo, w4, tn_ext, split_time=False):
  n = hdn_hi.shape[1]
  c = w4.shape[1] // (2 * HY_ORDER)
  tc = 64
  w4t = w4.T.reshape(2, HY_ORDER, c, HY_HID)
  w4h = w4t.astype(BF16)
  w4l = (w4t - w4h.astype(F32)).astype(BF16)
  deltas = jnp.abs(jnp.linspace(math.log(HY_DECAY_TARGET) / HY_FAST_DECAY,
                                math.log(HY_DECAY_TARGET) / HY_SLOW_DECAY, c, dtype=F32)).reshape(c, 1)
  wspec = lambda d: pl.BlockSpec((None, None, tc, HY_HID), lambda o, i: (d, o, i, 0))
  hspec = pl.BlockSpec((HY_HID, n), lambda o, i: (0, 0))
  nct = c // tc
  if split_time:
    out_spec = pl.BlockSpec((tc, n // DFT_R, DFT_R), lambda o, i: (o * nct + i, 0, 0))
    out_shape = jax.ShapeDtypeStruct((HY_ORDER * c, n // DFT_R, DFT_R), F32)
  else:
    out_spec = pl.BlockSpec((tc, n), lambda o, i: (o * nct + i, 0))
    out_shape = jax.ShapeDtypeStruct((HY_ORDER * c, n), F32)
  return pl.pallas_call(
      _hy_filter_kernel,
      grid=(HY_ORDER, nct),
      in_specs=[hspec, hspec, wspec(0), wspec(0), wspec(1), wspec(1),
                pl.BlockSpec((1, n), lambda o, i: (0, 0)),
                pl.BlockSpec((tc, 1), lambda o, i: (i, 0))],
      out_specs=out_spec,
      out_shape=out_shape,
      compiler_params=_cparams(("parallel", "parallel")),
      name="hy_filter",
  )(hdn_hi, hdn_lo, w4h, w4l, w4h, w4l, tn_ext, deltas)


def _short_conv_cm(x, cw, period):
  width = x.shape[1]
  assert period & (period - 1) == 0
  t = jnp.bitwise_and(lax.broadcasted_iota(jnp.int32, x.shape, 1), period - 1)
  prev = jnp.where(t == 0, 0.0, pltpu.roll(x, 1, 1))
  nxt = jnp.where(t == period - 1, 0.0, pltpu.roll(x, width - 1, 1))
  return cw[:, 3:4] + cw[:, 0:1] * prev + cw[:, 1:2] * x + cw[:, 2:3] * nxt


def _dft_stage_a(x, fa, tc, ts):
  a = jnp.dot(fa, x.astype(BF16), preferred_element_type=F32)
  ar, ai = a[:DFT_R], a[DFT_R:]
  return jnp.concatenate([ar * tc + ai * ts, ai * tc - ar * ts], axis=1).astype(BF16)


def _hyena_lat_kernel(skip_ref, g0_ref, g1_ref, z_ref, cw0_ref, cw1_ref, cwz_ref, k0_ref, k1_ref,
                      fah_ref, faf_ref, tw_ref, gc_ref, gci_ref, fai_ref, o_ref, h0_ref, h1_ref,
                      *, n_ch):
  i = pl.program_id(0)
  b = pl.program_id(1)
  tc, ts = tw_ref[0], tw_ref[1]
  rows = HY_CG * DFT_R
  half = SEQ // DFT_R
  lane = lax.broadcasted_iota(jnp.int32, (half, DFT_R), 1)
  row = lax.broadcasted_iota(jnp.int32, (half, DFT_R), 0)
  first_lane, last_lane = lane == 0, lane == DFT_R - 1
  first_row, last_row = row == 0, row == half - 1

  def short_conv(x, cw):
    back = pltpu.roll(x, 1, 1)
    wrap = jnp.where(first_row, 0.0, pltpu.roll(back, 1, 0))
    prev = jnp.where(first_lane, wrap, back)
    fwd = pltpu.roll(x, DFT_R - 1, 1)
    wrap = jnp.where(last_row, 0.0, pltpu.roll(fwd, half - 1, 0))
    nxt = jnp.where(last_lane, wrap, fwd)
    return cw[3:4] + cw[0:1] * prev + cw[1:2] * x + cw[2:3] * nxt

  @pl.when(b == 0)
  def _():
    def spectrum(g, carry):
      base = g * HY_CG
      for k_ref, h_ref in ((k0_ref, h0_ref), (k1_ref, h1_ref)):
        ap = jnp.concatenate([_dft_stage_a(k_ref[base + c], faf_ref[...], tc, ts)
                              for c in range(HY_CG)], axis=0)
        h_ref[pl.ds(pl.multiple_of(g * rows, rows), rows), :] = jnp.dot(
            ap, gc_ref[...], preferred_element_type=F32)
      return carry
    lax.fori_loop(0, HY_CT // HY_CG, spectrum, 0)

  def group(g, carry):
    base = g * HY_CG
    zs = [short_conv(z_ref[base + c], cwz_ref[base + c]) for c in range(HY_CG)]
    for o, (h_ref, gate_ref, cw_ref) in enumerate(((h0_ref, g0_ref, cw0_ref),
                                                   (h1_ref, g1_ref, cw1_ref))):
      ap = jnp.concatenate([_dft_stage_a(zs[c], fah_ref[...], tc, ts) for c in range(HY_CG)], axis=0)
      x = jnp.dot(ap, gc_ref[...], preferred_element_type=F32)
      h = h_ref[pl.ds(pl.multiple_of(g * rows, rows), rows), :]
      xr, xi, hr, hi = x[:, :DFT_R], x[:, DFT_R:], h[:, :DFT_R], h[:, DFT_R:]
      y = jnp.concatenate([xr * hr - xi * hi, xr * hi + xi * hr], axis=1).astype(BF16)
      bb = jnp.dot(y, gci_ref[...], preferred_element_type=F32)
      nxt = []
      for c in range(HY_CG):
        bc = bb[c * DFT_R:(c + 1) * DFT_R]
        br, bi = bc[:, :DFT_R], bc[:, DFT_R:]
        st = jnp.concatenate([br * tc - bi * ts, br * ts + bi * tc], axis=0).astype(BF16)
        conv = jnp.dot(fai_ref[...], st, preferred_element_type=F32)
        sk = skip_ref[o * n_ch + i * HY_CT + base + c]
        gate = short_conv(gate_ref[base + c], cw_ref[base + c])
        nxt.append(gate * (conv + sk * zs[c]))
      zs = nxt
    for c in range(HY_CG):
      o_ref[base + c] = zs[c].astype(o_ref.dtype)
    return carry
  lax.fori_loop(0, HY_CT // HY_CG, group, 0)


def hyena_lat(u3, cw, k3, skip, consts):
  c = u3.shape[0] // 3
  nct = c // HY_CT
  half = SEQ // DFT_R
  cwx = jnp.broadcast_to(cw[:, :, None], (3 * c, 4, DFT_R))
  uspec = lambda part: pl.BlockSpec((HY_CT, half, DFT_R), lambda i, b, sk: (part * nct + i, b, 0))
  wspec = lambda part: pl.BlockSpec((HY_CT, 4, DFT_R), lambda i, b, sk: (part * nct + i, 0, 0))
  kspec = lambda o: pl.BlockSpec((HY_CT, DFT_R, DFT_R), lambda i, b, sk: (o * nct + i, 0, 0))
  cspec = lambda a: pl.BlockSpec(a.shape, lambda i, b, sk: (0,) * a.ndim)
  cs = [consts[k] for k in ("fa_half", "fa_full", "tw", "gc", "gci", "fai")]
  out = pl.pallas_call(
      functools.partial(_hyena_lat_kernel, n_ch=c),
      grid_spec=pltpu.PrefetchScalarGridSpec(
          num_scalar_prefetch=1,
          grid=(nct, BATCH),
          in_specs=[uspec(0), uspec(1), uspec(2), wspec(0), wspec(1), wspec(2), kspec(0), kspec(1)]
          + [cspec(a) for a in cs],
          out_specs=pl.BlockSpec((HY_CT, half, DFT_R), lambda i, b, sk: (i, b, 0)),
          scratch_shapes=[pltpu.VMEM((HY_CT * DFT_R, 2 * DFT_R), F32)] * 2,
      ),
      out_shape=jax.ShapeDtypeStruct((c, N_LAT // DFT_R, DFT_R), BF16),
      compiler_params=_cparams(("parallel", "arbitrary")),
      name="hyena_lat",
  )(skip.reshape(-1), u3, u3, u3, cwx, cwx, cwx, k3, k3, *cs)
  return out.reshape(c, N_LAT)


def _hyena_ctx_kernel(g0_ref, g1_ref, z_ref, cw0_ref, cw1_ref, cwz_ref, k0_ref, k1_ref, sk_ref,
                      fwd_ref, fwdh_ref, inv_ref, o_ref):
  n = HY_N_CTX
  flat = lambda ref: jnp.concatenate([ref[:, a, :] for a in range(N_CTX // DFT_R)], axis=1)
  g0 = _short_conv_cm(flat(g0_ref), cw0_ref[...], CTX_LEN)
  g1 = _short_conv_cm(flat(g1_ref), cw1_ref[...], CTX_LEN)
  z = _short_conv_cm(flat(z_ref), cwz_ref[...], CTX_LEN)
  sk = sk_ref[...]
  spec = [jnp.dot(k_ref[...].astype(BF16), fwd_ref[...], preferred_element_type=F32)
          for k_ref in (k0_ref, k1_ref)]
  for b in range(BATCH):
    cols = slice(b * CTX_LEN, (b + 1) * CTX_LEN)
    zb = z[:, cols]
    for o, gate in enumerate((g0, g1)):
      x = jnp.dot(zb.astype(BF16), fwdh_ref[...], preferred_element_type=F32)
      h = spec[o]
      xr, xi, hr, hi = x[:, :n], x[:, n:], h[:, :n], h[:, n:]
      y = jnp.concatenate([xr * hr - xi * hi, xr * hi + xi * hr], axis=1).astype(BF16)
      conv = jnp.dot(y, inv_ref[...], preferred_element_type=F32)
      zb = gate[:, cols] * (conv + sk[:, o:o + 1] * zb)
    o_ref[:, cols] = zb.astype(o_ref.dtype)
  o_ref[:, N_CTX:] = jnp.zeros((o_ref.shape[0], TAIL_ROWS - N_CTX), o_ref.dtype)


def hyena_ctx(u3, cw, kern_t, skip, consts):
  c = u3.shape[0] // 3
  tc = 256 if c % 256 == 0 else c
  nct = c // tc
  tail = TAIL_ROWS // DFT_R
  blk = CTX_ROW0 // TAIL_ROWS
  xspec = lambda part: pl.BlockSpec((tc, tail, DFT_R), lambda i: (part * nct + i, blk, 0))
  wspec = lambda part: pl.BlockSpec((tc, 4), lambda i: (part * nct + i, 0))
  kspec = lambda o: pl.BlockSpec((tc, HY_N_CTX), lambda i: (o * nct + i, 0))
  cspec = lambda a: pl.BlockSpec(a.shape, lambda i: (0,) * a.ndim)
  cs = [consts[k] for k in ("fwd", "fwd_half", "inv")]
  return pl.pallas_call(
      _hyena_ctx_kernel,
      grid=(nct,),
      in_specs=[xspec(0), xspec(1), xspec(2), wspec(0), wspec(1), wspec(2), kspec(0), kspec(1),
                pl.BlockSpec((tc, HY_ORDER), lambda i: (i, 0))] + [cspec(a) for a in cs],
      out_specs=pl.BlockSpec((tc, TAIL_ROWS), lambda i: (i, 0)),
      out_shape=jax.ShapeDtypeStruct((c, TAIL_ROWS), BF16),
      compiler_params=_cparams(("parallel",)),
      name="hyena_ctx",
  )(u3, u3, u3, cw, cw, cw, kern_t, kern_t, skip.T, *cs)


def hyena_cm(u3, conv_w, conv_b, w1, b1, w2, b2, w3, b3, w4, freq, skip):
  cw = jnp.concatenate([conv_w.T, conv_b[:, None]], axis=1)
  outs = []
  for length, lat in ((SEQ, True), (CTX_LEN, False)):
    feats_t, tn_ext = _hy_positions(length)
    hdn_hi, hdn_lo = hy_mlp(feats_t, w1, b1, w2, b2, w3, b3, freq)
    kern = hy_filter(hdn_hi, hdn_lo, w4, tn_ext, split_time=lat)
    if lat:
      outs.append(hyena_lat(u3, cw, kern, skip, _dft_consts_lat()))
    else:
      outs.append(hyena_ctx(u3, cw, kern, skip, _dft_consts_ctx()))
  return tuple(outs)


def _mm_nt_kernel(x_ref, wt_ref, o_ref):
  acc = lax.dot_general(wt_ref[...], x_ref[...], (((1,), (1,)), ((), ())),
                        preferred_element_type=F32)
  for a in range(o_ref.shape[1]):
    o_ref[:, a, :] = acc[:, a * DFT_R:(a + 1) * DFT_R]


def mm_nt(x, w_t, layer, tn=1024):
  _, n, k = w_t.shape
  return pl.pallas_call(
      _mm_nt_kernel,
      grid=(n // tn, T_PAD // TM),
      in_specs=[_xspec(k), pl.BlockSpec((None, tn, k), lambda j, i: (layer, j, 0))],
      out_specs=pl.BlockSpec((tn, TM // DFT_R, DFT_R), lambda j, i: (j, i, 0)),
      out_shape=jax.ShapeDtypeStruct((n, T_PAD // DFT_R, DFT_R), F32),
      compiler_params=_cparams(("parallel", "parallel")),
      name="mm_nt",
  )(x, w_t)


def _mm_resid2_kernel(al_ref, ac_ref, zl_ref, zc_ref, wa_ref, wz_ref, s_ref, m_ref, o_ref,
                      *, n_lat_tiles):
  i = pl.program_id(1)

  def mix(a_ref, zt_ref):
    acc = jnp.dot(a_ref[...], wa_ref[...], preferred_element_type=F32)
    acc += lax.dot_general(zt_ref[...], wz_ref[...], (((0,), (0,)), ((), ())),
                           preferred_element_type=F32)
    o_ref[...] = s_ref[...] + m_ref[...] * acc

  @pl.when(i < n_lat_tiles)
  def _():
    mix(al_ref, zl_ref)

  @pl.when(i >= n_lat_tiles)
  def _():
    mix(ac_ref, zc_ref)


def mm_resid2(a_lat, a_tail, z_lat_t, z_tail_t, w, layer, s, gate8, tn=512):
  ka, kz = a_lat.shape[1], z_lat_t.shape[0]
  n = w.shape[2]
  n_lat_tiles = N_LAT // TM
  assert ka == kz and a_tail.shape[0] == TM and z_tail_t.shape[1] == TM
  row = _mod_row(TM)
  lat = lambda i: jnp.minimum(i, n_lat_tiles - 1)
  once = pl.Buffered(1)
  return pl.pallas_call(
      functools.partial(_mm_resid2_kernel, n_lat_tiles=n_lat_tiles),
      grid=(n // tn, T_PAD // TM),
      in_specs=[
          pl.BlockSpec((TM, ka), lambda j, i: (lat(i), 0)),
          pl.BlockSpec((TM, ka), lambda j, i: (0, 0), pipeline_mode=once),
          pl.BlockSpec((kz, TM), lambda j, i: (0, lat(i))),
          pl.BlockSpec((kz, TM), lambda j, i: (0, 0), pipeline_mode=once),
          pl.BlockSpec((None, ka, tn), lambda j, i: (layer, 0, j)),
          pl.BlockSpec((None, kz, tn), lambda j, i: (layer, 1, j)),
          pl.BlockSpec((TM, tn), lambda j, i: (i, j)),
          pl.BlockSpec((None, 1, tn), lambda j, i: (row(i), 0, j)),
      ],
      out_specs=pl.BlockSpec((TM, tn), lambda j, i: (i, j)),
      out_shape=jax.ShapeDtypeStruct((T_PAD, n), F32),
      input_output_aliases={6: 0},
      compiler_params=_cparams(("parallel", "parallel")),
      name="mm_resid2",
  )(a_lat, a_tail, z_lat_t, z_tail_t, w, w, s, gate8)


def kernel(x, c, ctx, c_ctx, ada_a, ada_b, ada_bias, norm_mix, norm_ffn, even_w_in, na_q_gain,
           na_k_gain, na_rpb, hy_conv_w, hy_conv_b, hy_w1, hy_b1, hy_w2, hy_b2, hy_w3, hy_b3,
           hy_w4, hy_freq, hy_skip, odd_w_in, odd_b_in, gm_v_gain, gm_w_s, gm_b_s, w_mix_out,
           moe_router, moe_bias, moe_w_gate, moe_w_up, moe_w_down, sh_w_gate, sh_w_up, sh_w_down):
  s = jnp.concatenate([x.reshape(N_LAT, D_MODEL), ctx.reshape(N_CTX, D_MODEL),
                       jnp.zeros((T_PAD - T_REAL, D_MODEL), F32)], axis=0)
  cond8 = jnp.concatenate([c, c_ctx[None], jnp.zeros((8 - BATCH - 1, D_MODEL), F32)], axis=0)
  mods = ada_table(cond8, ada_a, ada_b, ada_bias).reshape(DEPTH, 8, N_MOD, 1, D_MODEL)
  w_qkv = even_w_in[:, :, :3 * NA_WIDTH].astype(BF16)
  w_hy_t = even_w_in[:, :, 3 * NA_WIDTH:].transpose(0, 2, 1).astype(BF16)
  w_odd = odd_w_in.astype(BF16)
  w_out = w_mix_out.astype(BF16)
  wsg, wsu = sh_w_gate.astype(BF16), sh_w_up.astype(BF16)
  wd_r = moe_w_down.reshape(DEPTH, N_EXPERTS * EXPERT_DIM, D_MODEL).astype(BF16)
  wd_s = sh_w_down.astype(BF16)

  for layer in range(DEPTH):
    j = layer // 2
    rows = N_LAT if layer == DEPTH - 1 else T_PAD
    mod = [mods[layer, :, i] for i in range(N_MOD)]
    if layer % 2 == 0:
      h = norm_mod(s, norm_mix[layer], mod[0], mod[1])
      gains = jnp.concatenate([jnp.tile(na_q_gain[j], NA_HEADS), jnp.tile(na_k_gain[j], NA_HEADS),
                               jnp.ones((NA_WIDTH,), F32)]).reshape(1, 3 * NA_WIDTH)
      qkv = mm_headnorm(h, w_qkv, j, gains, 3 * NA_WIDTH, 2 * NA_WIDTH)
      hy3 = mm_nt(h, w_hy_t, j)
      a_lat = na_attention(qkv, na_bias(na_rpb[j]))
      a_tail = ctx_attention(qkv)
      z_lat_t, z_tail_t = hyena_cm(hy3, hy_conv_w[j], hy_conv_b[j], hy_w1[j], hy_b1[j], hy_w2[j],
                                   hy_b2[j], hy_w3[j], hy_b3[j], hy_w4[j], hy_freq[j], hy_skip[j])
      s = mm_resid2(a_lat, a_tail, z_lat_t, z_tail_t, w_out, layer, s, mod[2])
    else:
      h = norm_mod(s, norm_mix[layer], mod[0], mod[1], rows)
      z = mm_gelu(h, w_odd, j, odd_b_in[j], rows)
      bsx = jnp.repeat(gm_b_s[j].T, HEAD_DIM, axis=1)
      mix = gmlp_gate(z, gm_v_gain[j], gm_w_s[j].astype(BF16), bsx)
      s = mm_resid(mix, w_out, layer, s, mod[2], rows)

    f, gates = norm_mod_router(s, norm_ffn[layer], mod[3], mod[4], moe_router[layer], moe_bias[layer],
                               rows)
    gates_t = gates[:, :N_EXPERTS].reshape(rows, N_EXPERTS // EXP_PER_TILE, EXP_PER_TILE)
    gates_t = gates_t.transpose(1, 0, 2)
    hid_r = moe_up(f, moe_w_gate, moe_w_up, layer, gates_t, rows)
    hid_s = moe_up_shared(f, wsg, wsu, layer, rows)
    s = moe_down(hid_r, hid_s, wd_r, wd_s, layer, s, mod[5], rows)

  return s.reshape(BATCH, SEQ, D_MODEL)
```

```python
import functools
import math

import numpy as np
import jax
import jax.numpy as jnp
from jax import lax
from jax.experimental import pallas as pl
from jax.experimental.pallas import tpu as pltpu

F32 = jnp.float32
BF16 = jnp.bfloat16

D_MODEL = 4096
BATCH = 2
SEQ = 8192
DEPTH = 4
GRID_W = 64
ROWS = SEQ // GRID_W
CTX_LEN = 256
HEAD_DIM = 128
NA_HEADS = 16
NA_WIDTH = NA_HEADS * HEAD_DIM
HY_WIDTH = D_MODEL - NA_WIDTH
WIN_H = 8
WIN_W = 16
HY_ORDER = 2
HY_SHORT = 3
HY_BANDS = 16
HY_DECAY_TARGET = 1e-2
HY_FAST_DECAY = 0.3
HY_SLOW_DECAY = 1.5
GM_WIDTH = D_MODEL
GM_GROUPS = GM_WIDTH // HEAD_DIM
CHUNK = 128
N_EXPERTS = 64
TOP_K = 8
EXPERT_DIM = 128
SHARED_DIM = TOP_K * EXPERT_DIM
ROUTED_SCALE = 2.5
N_MOD = 6
EPS = 1e-6

LANES = 128
VMEM_LIMIT = 56 * 1024 * 1024

TM = 1024
N_LAT = BATCH * SEQ
N_CTX = BATCH * CTX_LEN
T_REAL = N_LAT + N_CTX
T_PAD = -(-T_REAL // TM) * TM
CTX_ROW0 = N_LAT
TAIL_ROWS = T_PAD - N_LAT
assert TAIL_ROWS == TM
TN_NORM = 512
MOE_TN = 512
EXP_PER_TILE = MOE_TN // EXPERT_DIM
NEG = -1e30

NA_R = 4
NA_Q = NA_R * GRID_W
NA_KROWS = NA_R + WIN_H
NA_KTOK = NA_KROWS * GRID_W
NA_QBLOCKS = ROWS // NA_R
NA_HP = 2


def _mod_row(tile_rows):
  def f(i):
    return jnp.minimum((i * tile_rows) // SEQ, BATCH)
  return f


def _cparams(sem):
  return pltpu.CompilerParams(dimension_semantics=sem, vmem_limit_bytes=VMEM_LIMIT)


def _ada_kernel(cond_ref, a_ref, b_ref, bias_ref, o_ref):
  cnd = cond_ref[...]
  sc = cnd * jax.nn.sigmoid(cnd)
  t = jnp.dot(sc, a_ref[...], preferred_element_type=F32, precision=lax.Precision.HIGHEST)
  o_ref[...] = jnp.dot(t, b_ref[...], preferred_element_type=F32,
                       precision=lax.Precision.HIGHEST) + bias_ref[...]


def ada_table(cond8, ada_a, ada_b, ada_bias):
  tn = 4096
  rank = ada_a.shape[-1]
  nmd = N_MOD * D_MODEL
  return pl.pallas_call(
      _ada_kernel,
      grid=(DEPTH, nmd // tn),
      in_specs=[
          pl.BlockSpec((8, D_MODEL), lambda l, j: (0, 0)),
          pl.BlockSpec((None, D_MODEL, rank), lambda l, j: (l, 0, 0)),
          pl.BlockSpec((None, rank, tn), lambda l, j: (l, 0, j)),
          pl.BlockSpec((None, 1, tn), lambda l, j: (l, 0, j)),
      ],
      out_specs=pl.BlockSpec((None, 8, tn), lambda l, j: (l, 0, j)),
      out_shape=jax.ShapeDtypeStruct((DEPTH, 8, nmd), F32),
      compiler_params=_cparams(("arbitrary", "arbitrary")),
      name="ada_table",
  )(cond8, ada_a, ada_b, ada_bias.reshape(DEPTH, 1, nmd))


def _norm_mod(x, g, shift, scale):
  ms = jnp.mean(x * x, axis=-1, keepdims=True)
  y = x * lax.rsqrt(ms + EPS) * g
  return y * (1.0 + scale) + shift


def _norm_kernel(x_ref, g_ref, shift_ref, scale_ref, o_ref):
  o_ref[...] = _norm_mod(x_ref[...], g_ref[...], shift_ref[...], scale_ref[...]).astype(BF16)


def _norm_router_kernel(x_ref, g_ref, shift_ref, scale_ref, whi_ref, wlo_ref, rb_ref,
                        o_ref, gates_ref):
  f = _norm_mod(x_ref[...], g_ref[...], shift_ref[...], scale_ref[...])
  f_hi = f.astype(BF16)
  o_ref[...] = f_hi
  f_lo = (f - f_hi.astype(F32)).astype(BF16)
  w_hi = whi_ref[...]
  logits = (jnp.dot(f_hi, w_hi, preferred_element_type=F32)
            + jnp.dot(f_lo, w_hi, preferred_element_type=F32)
            + jnp.dot(f_hi, wlo_ref[...], preferred_element_type=F32))
  scores = jax.nn.sigmoid(logits)
  lane = lax.broadcasted_iota(jnp.int32, scores.shape, 1).astype(F32)
  work = jnp.where(lane < N_EXPERTS, scores + rb_ref[...], NEG)
  sel = jnp.zeros(scores.shape, F32)
  for _ in range(TOP_K):
    m = jnp.max(work, axis=-1, keepdims=True)
    first = jnp.min(jnp.where(work == m, lane, float(LANES)), axis=-1, keepdims=True)
    hit = lane == first
    sel = jnp.where(hit, scores, sel)
    work = jnp.where(hit, 3.0 * NEG, work)
  gates_ref[...] = sel / jnp.sum(sel, axis=-1, keepdims=True) * ROUTED_SCALE


def norm_mod(s, gain, shift8, scale8, rows=T_PAD):
  n = rows // TN_NORM
  row = _mod_row(TN_NORM)
  return pl.pallas_call(
      _norm_kernel,
      grid=(n,),
      in_specs=[
          pl.BlockSpec((TN_NORM, D_MODEL), lambda i: (i, 0)),
          pl.BlockSpec((1, D_MODEL), lambda i: (0, 0)),
          pl.BlockSpec((None, 1, D_MODEL), lambda i: (row(i), 0, 0)),
          pl.BlockSpec((None, 1, D_MODEL), lambda i: (row(i), 0, 0)),
      ],
      out_specs=pl.BlockSpec((TN_NORM, D_MODEL), lambda i: (i, 0)),
      out_shape=jax.ShapeDtypeStruct((rows, D_MODEL), BF16),
      compiler_params=_cparams(("parallel",)),
      name="norm_mod",
  )(s, gain.reshape(1, D_MODEL), shift8, scale8)


def norm_mod_router(s, gain, shift8, scale8, w_router, r_bias, rows=T_PAD):
  n = rows // TN_NORM
  row = _mod_row(TN_NORM)
  w_pad = jnp.pad(w_router, ((0, 0), (0, LANES - N_EXPERTS)))
  w_hi = w_pad.astype(BF16)
  w_lo = (w_pad - w_hi.astype(F32)).astype(BF16)
  rb = jnp.pad(r_bias.astype(F32), (0, LANES - N_EXPERTS)).reshape(1, LANES)
  return pl.pallas_call(
      _norm_router_kernel,
      grid=(n,),
      in_specs=[
          pl.BlockSpec((TN_NORM, D_MODEL), lambda i: (i, 0)),
          pl.BlockSpec((1, D_MODEL), lambda i: (0, 0)),
          pl.BlockSpec((None, 1, D_MODEL), lambda i: (row(i), 0, 0)),
          pl.BlockSpec((None, 1, D_MODEL), lambda i: (row(i), 0, 0)),
          pl.BlockSpec((D_MODEL, LANES), lambda i: (0, 0)),
          pl.BlockSpec((D_MODEL, LANES), lambda i: (0, 0)),
          pl.BlockSpec((1, LANES), lambda i: (0, 0)),
      ],
      out_specs=[
          pl.BlockSpec((TN_NORM, D_MODEL), lambda i: (i, 0)),
          pl.BlockSpec((TN_NORM, LANES), lambda i: (i, 0)),
      ],
      out_shape=[
          jax.ShapeDtypeStruct((rows, D_MODEL), BF16),
          jax.ShapeDtypeStruct((rows, LANES), F32),
      ],
      compiler_params=_cparams(("parallel",)),
      name="norm_mod_router",
  )(s, gain.reshape(1, D_MODEL), shift8, scale8, w_hi, w_lo, rb)


def _gelu_exact(x):
  return 0.5 * x * (1.0 + lax.erf(x * (1.0 / math.sqrt(2.0))))


MXU_N = 256


def _col_chunks(width):
  return [slice(c, c + MXU_N) for c in range(0, width, MXU_N)]


def _mm_headnorm_kernel(x_ref, w_ref, g_ref, o_ref, *, n_norm_tiles):
  j = pl.program_id(0)
  x = x_ref[...]

  @pl.when(j < n_norm_tiles)
  def _():
    for cs in _col_chunks(o_ref.shape[1]):
      acc = jnp.dot(x, w_ref[:, cs], preferred_element_type=F32)
      for h in range(MXU_N // HEAD_DIM):
        blk = acc[:, h * HEAD_DIM:(h + 1) * HEAD_DIM]
        hs = slice(cs.start + h * HEAD_DIM, cs.start + (h + 1) * HEAD_DIM)
        ms = jnp.mean(blk * blk, axis=-1, keepdims=True)
        o_ref[:, hs] = (blk * lax.rsqrt(ms + EPS) * g_ref[:, hs]).astype(o_ref.dtype)

  @pl.when(j >= n_norm_tiles)
  def _():
    o_ref[...] = jnp.dot(x, w_ref[...], preferred_element_type=F32).astype(o_ref.dtype)


def _mm_gelu_kernel(x_ref, w_ref, b_ref, o_ref):
  x = x_ref[...]
  for cs in _col_chunks(o_ref.shape[1]):
    acc = jnp.dot(x, w_ref[:, cs], preferred_element_type=F32) + b_ref[:, cs]
    o_ref[:, cs] = _gelu_exact(acc).astype(o_ref.dtype)


def _mm_resid_kernel(x_ref, w_ref, s_ref, m_ref, o_ref):
  acc = jnp.dot(x_ref[...], w_ref[...], preferred_element_type=F32)
  o_ref[...] = s_ref[...] + m_ref[...] * acc


def _moe_down_kernel(hr_ref, hs_ref, wr_ref, ws_ref, s_ref, m_ref, o_ref, acc_ref):
  k = pl.program_id(2)

  @pl.when(k == 0)
  def _():
    acc_ref[...] = jnp.zeros_like(acc_ref)

  acc_ref[...] += jnp.dot(hr_ref[...], wr_ref[...], preferred_element_type=F32)

  @pl.when(k == pl.num_programs(2) - 1)
  def _():
    acc = acc_ref[...] + jnp.dot(hs_ref[...], ws_ref[...], preferred_element_type=F32)
    o_ref[...] = s_ref[...] + m_ref[...] * acc


def _swiglu(x, wg, wu):
  g = jnp.dot(x, wg, preferred_element_type=F32)
  u = jnp.dot(x, wu, preferred_element_type=F32)
  return g * jax.nn.sigmoid(g) * u


def _moe_up_kernel(x_ref, wg_ref, wu_ref, gate_ref, o_ref, wgb_ref, wub_ref):
  @pl.when(pl.program_id(1) == 0)
  def _():
    for e in range(EXP_PER_TILE):
      sl = slice(e * EXPERT_DIM, (e + 1) * EXPERT_DIM)
      wgb_ref[:, sl] = wg_ref[e].astype(BF16)
      wub_ref[:, sl] = wu_ref[e].astype(BF16)

  x = x_ref[...]
  gate = gate_ref[...]
  for cs in _col_chunks(o_ref.shape[1]):
    hid = _swiglu(x, wgb_ref[:, cs], wub_ref[:, cs])
    for e in range(MXU_N // EXPERT_DIM):
      col = cs.start + e * EXPERT_DIM
      g = gate[:, col // EXPERT_DIM:col // EXPERT_DIM + 1]
      o_ref[:, col:col + EXPERT_DIM] = (hid[:, e * EXPERT_DIM:(e + 1) * EXPERT_DIM] * g).astype(o_ref.dtype)


def _moe_up_shared_kernel(x_ref, wg_ref, wu_ref, o_ref):
  x = x_ref[...]
  for cs in _col_chunks(o_ref.shape[1]):
    o_ref[:, cs] = _swiglu(x, wg_ref[:, cs], wu_ref[:, cs]).astype(o_ref.dtype)


def _xspec(k):
  return pl.BlockSpec((TM, k), lambda j, i: (i, 0))


def _wspec(k, tn, layer, col0=0):
  return pl.BlockSpec((None, k, tn), lambda j, i: (layer, 0, col0 // tn + j))


def mm_headnorm(x, w, layer, gains, n, n_norm_cols, tn=1024):
  k = w.shape[1]
  return pl.pallas_call(
      functools.partial(_mm_headnorm_kernel, n_norm_tiles=n_norm_cols // tn),
      grid=(n // tn, T_PAD // TM),
      in_specs=[_xspec(k), _wspec(k, tn, layer), pl.BlockSpec((1, tn), lambda j, i: (0, j))],
      out_specs=pl.BlockSpec((TM, tn), lambda j, i: (i, j)),
      out_shape=jax.ShapeDtypeStruct((T_PAD, n), BF16),
      compiler_params=_cparams(("parallel", "parallel")),
      name="mm_headnorm",
  )(x, w, gains)


def mm_gelu(x, w, layer, b, rows=T_PAD, tn=1024):
  _, k, n = w.shape
  return pl.pallas_call(
      _mm_gelu_kernel,
      grid=(n // tn, rows // TM),
      in_specs=[_xspec(k), _wspec(k, tn, layer), pl.BlockSpec((1, tn), lambda j, i: (0, j))],
      out_specs=pl.BlockSpec((TM, tn), lambda j, i: (i, j)),
      out_shape=jax.ShapeDtypeStruct((rows, n), BF16),
      compiler_params=_cparams(("parallel", "parallel")),
      name="mm_gelu",
  )(x, w, b.reshape(1, n))


def mm_resid(x, w, layer, s, gate8, rows=T_PAD, tn=512):
  _, k, n = w.shape
  row = _mod_row(TM)
  return pl.pallas_call(
      _mm_resid_kernel,
      grid=(n // tn, rows // TM),
      in_specs=[_xspec(k), _wspec(k, tn, layer),
                pl.BlockSpec((TM, tn), lambda j, i: (i, j)),
                pl.BlockSpec((None, 1, tn), lambda j, i: (row(i), 0, j))],
      out_specs=pl.BlockSpec((TM, tn), lambda j, i: (i, j)),
      out_shape=jax.ShapeDtypeStruct(s.shape, F32),
      input_output_aliases={2: 0},
      compiler_params=_cparams(("parallel", "parallel")),
      name="mm_resid",
  )(x, w, s, gate8)


def moe_down(hid_r, hid_s, wd_r, wd_s, layer, s, gate8, out_rows=T_PAD, tn=1024, tk=2048):
  _, kr, n = wd_r.shape
  ks = wd_s.shape[1]
  row = _mod_row(TM)
  return pl.pallas_call(
      _moe_down_kernel,
      grid=(n // tn, out_rows // TM, kr // tk),
      in_specs=[
          pl.BlockSpec((TM, tk), lambda j, i, kk: (i, kk)),
          pl.BlockSpec((TM, ks), lambda j, i, kk: (i, 0)),
          pl.BlockSpec((None, tk, tn), lambda j, i, kk: (layer, kk, j)),
          pl.BlockSpec((None, ks, tn), lambda j, i, kk: (layer, 0, j)),
          pl.BlockSpec((TM, tn), lambda j, i, kk: (i, j)),
          pl.BlockSpec((None, 1, tn), lambda j, i, kk: (row(i), 0, j)),
      ],
      out_specs=pl.BlockSpec((TM, tn), lambda j, i, kk: (i, j)),
      out_shape=jax.ShapeDtypeStruct((out_rows, n), F32),
      scratch_shapes=[pltpu.VMEM((TM, tn), F32)],
      input_output_aliases={4: 0} if out_rows == T_PAD else {},
      compiler_params=_cparams(("parallel", "parallel", "arbitrary")),
      name="moe_down",
  )(hid_r, hid_s, wd_r, wd_s, s, gate8)


def moe_up(x, w_gate, w_up, layer, gates_t, rows=T_PAD):
  _, n_exp, k, f = w_gate.shape
  tn = MOE_TN
  wspec = pl.BlockSpec((None, EXP_PER_TILE, k, f), lambda j, i: (layer, j, 0, 0),
                       pipeline_mode=pl.Buffered(1))
  return pl.pallas_call(
      _moe_up_kernel,
      grid=(n_exp // EXP_PER_TILE, rows // TM),
      in_specs=[_xspec(k), wspec, wspec,
                pl.BlockSpec((None, TM, EXP_PER_TILE), lambda j, i: (j, i, 0))],
      out_specs=pl.BlockSpec((TM, tn), lambda j, i: (i, j)),
      out_shape=jax.ShapeDtypeStruct((rows, n_exp * f), BF16),
      scratch_shapes=[pltpu.VMEM((k, tn), BF16)] * 2,
      compiler_params=_cparams(("parallel", "arbitrary")),
      name="moe_up",
  )(x, w_gate, w_up, gates_t)


def moe_up_shared(x, wg, wu, layer, rows=T_PAD):
  _, k, n = wg.shape
  tn = MOE_TN
  return pl.pallas_call(
      _moe_up_shared_kernel,
      grid=(n // tn, rows // TM),
      in_specs=[_xspec(k), _wspec(k, tn, layer), _wspec(k, tn, layer)],
      out_specs=pl.BlockSpec((TM, tn), lambda j, i: (i, j)),
      out_shape=jax.ShapeDtypeStruct((rows, n), BF16),
      compiler_params=_cparams(("parallel", "parallel")),
      name="moe_up_shared",
  )(x, wg, wu)


def _na_kernel(q_ref, k_ref, v_ref, kc_ref, vc_ref, bias_ref, o_ref):
  qb = pl.program_id(2)
  kb = jnp.clip(qb * NA_R - WIN_H // 2, 0, ROWS - NA_KROWS)
  start = pl.multiple_of(kb * GRID_W, GRID_W)
  kind = jnp.where(qb == 0, 0, jnp.where(qb == NA_QBLOCKS - 1, 2, 1))
  scale = HEAD_DIM ** -0.5
  nt = (((1,), (1,)), ((), ()))
  for hh in range(NA_HP):
    cols = slice(hh * HEAD_DIM, (hh + 1) * HEAD_DIM)
    q = q_ref[:, cols]
    kw = k_ref[pl.ds(start, NA_KTOK), cols]
    vw = v_ref[pl.ds(start, NA_KTOK), cols]
    s = lax.dot_general(q, kw, nt, preferred_element_type=F32) * scale + bias_ref[kind, hh]
    sc = lax.dot_general(q, kc_ref[:, cols], nt, preferred_element_type=F32) * scale
    m = jnp.maximum(jnp.max(s, axis=-1, keepdims=True), jnp.max(sc, axis=-1, keepdims=True))
    p = jnp.exp(s - m)
    pc = jnp.exp(sc - m)
    den = jnp.sum(p, axis=-1, keepdims=True) + jnp.sum(pc, axis=-1, keepdims=True)
    o = (jnp.dot(p.astype(BF16), vw, preferred_element_type=F32)
         + jnp.dot(pc.astype(BF16), vc_ref[:, cols], preferred_element_type=F32))
    o_ref[:, cols] = (o / den).astype(o_ref.dtype)


def _na_bias_selectors():
  rows = np.zeros((3, NA_R, NA_KROWS, 2 * WIN_H - 1), np.float32)
  for kind, qb in enumerate((0, 1, NA_QBLOCKS - 1)):
    r0 = qb * NA_R
    kb = int(np.clip(r0 - WIN_H // 2, 0, ROWS - NA_KROWS))
    for i in range(NA_R):
      r = r0 + i
      rs = int(np.clip(r - WIN_H // 2, 0, ROWS - WIN_H))
      for j in range(NA_KROWS):
        if rs <= kb + j < rs + WIN_H:
          rows[kind, i, j, kb + j - r + WIN_H - 1] = 1.0
  cols = np.zeros((2 * WIN_W - 1, GRID_W, GRID_W), np.float32)
  for c in range(GRID_W):
    cs = int(np.clip(c - WIN_W // 2, 0, GRID_W - WIN_W))
    for d in range(cs, cs + WIN_W):
      cols[d - c + WIN_W - 1, c, d] = 1.0
  return rows, cols


def na_bias(rpb):
  rows, cols = _na_bias_selectors()
  hp = lax.Precision.HIGHEST
  colband = jnp.einsum('hrw,wcd->hrcd', rpb.astype(F32), cols, precision=hp)
  band = jnp.einsum('tijr,hrcd->thicjd', rows, colband, precision=hp)
  valid = np.einsum('tijr,wcd->ticjd', rows, cols) > 0.5
  band = jnp.where(valid[:, None], band, NEG)
  return band.reshape(3, NA_HEADS, NA_Q, NA_KTOK)


def na_attention(qkv, bias):
  hq = NA_HEADS // NA_HP
  width = NA_HP * HEAD_DIM
  lat_blocks = SEQ // NA_Q
  ctx_blk0 = CTX_ROW0 // CTX_LEN
  return pl.pallas_call(
      _na_kernel,
      grid=(BATCH, hq, NA_QBLOCKS),
      in_specs=[
          pl.BlockSpec((NA_Q, width), lambda b, h, i: (b * lat_blocks + i, h)),
          pl.BlockSpec((SEQ, width), lambda b, h, i: (b, hq + h)),
          pl.BlockSpec((SEQ, width), lambda b, h, i: (b, 2 * hq + h)),
          pl.BlockSpec((CTX_LEN, width), lambda b, h, i: (ctx_blk0 + b, hq + h)),
          pl.BlockSpec((CTX_LEN, width), lambda b, h, i: (ctx_blk0 + b, 2 * hq + h)),
          pl.BlockSpec((3, NA_HP, NA_Q, NA_KTOK), lambda b, h, i: (0, h, 0, 0)),
      ],
      out_specs=pl.BlockSpec((NA_Q, width), lambda b, h, i: (b * lat_blocks + i, h)),
      out_shape=jax.ShapeDtypeStruct((N_LAT, NA_WIDTH), BF16),
      compiler_params=_cparams(("parallel", "parallel", "arbitrary")),
      name="na_attention",
  )(qkv, qkv, qkv, qkv, qkv, bias)


def _ctx_attn_kernel(q_ref, k_ref, v_ref, o_ref):
  @pl.when(pl.program_id(0) < BATCH)
  def _():
    scale = HEAD_DIM ** -0.5
    s = lax.dot_general(q_ref[...], k_ref[...], (((1,), (1,)), ((), ())),
                        preferred_element_type=F32) * scale
    m = jnp.max(s, axis=-1, keepdims=True)
    p = jnp.exp(s - m)
    den = jnp.sum(p, axis=-1, keepdims=True)
    o = jnp.dot(p.astype(BF16), v_ref[...], preferred_element_type=F32)
    o_ref[...] = (o / den).astype(o_ref.dtype)

  @pl.when(pl.program_id(0) >= BATCH)
  def _():
    o_ref[...] = jnp.zeros_like(o_ref)


def ctx_attention(qkv):
  hq = NA_HEADS
  ctx_blk0 = CTX_ROW0 // CTX_LEN
  blk = lambda b: ctx_blk0 + jnp.minimum(b, BATCH - 1)
  return pl.pallas_call(
      _ctx_attn_kernel,
      grid=(TAIL_ROWS // CTX_LEN, NA_HEADS),
      in_specs=[
          pl.BlockSpec((CTX_LEN, HEAD_DIM), lambda b, h: (blk(b), h)),
          pl.BlockSpec((CTX_LEN, HEAD_DIM), lambda b, h: (blk(b), hq + h)),
          pl.BlockSpec((CTX_LEN, HEAD_DIM), lambda b, h: (blk(b), 2 * hq + h)),
      ],
      out_specs=pl.BlockSpec((CTX_LEN, HEAD_DIM), lambda b, h: (b, h)),
      out_shape=jax.ShapeDtypeStruct((TAIL_ROWS, NA_WIDTH), BF16),
      compiler_params=_cparams(("parallel", "parallel")),
      name="ctx_attention",
  )(qkv, qkv, qkv)


def _gmlp_gate_kernel(u_ref, v_ref, vg_ref, ws_ref, bs_ref, o_ref):
  v = v_ref[...].astype(F32)
  ms = jnp.mean(v * v, axis=-1, keepdims=True)
  vn = (v * lax.rsqrt(ms + EPS) * vg_ref[...]).astype(BF16)
  for c in range(v.shape[0] // CHUNK):
    rows = slice(c * CHUNK, (c + 1) * CHUNK)
    for g in range(GM_GROUPS):
      cols = slice(g * HEAD_DIM, (g + 1) * HEAD_DIM)
      sp = jnp.dot(ws_ref[g], vn[rows, cols], preferred_element_type=F32) + bs_ref[:, cols]
      o_ref[rows, cols] = (u_ref[rows, cols].astype(F32) * sp).astype(o_ref.dtype)


def gmlp_gate(z, v_gain, w_s, bsx):
  tg = 256
  rows = z.shape[0]
  return pl.pallas_call(
      _gmlp_gate_kernel,
      grid=(rows // tg,),
      in_specs=[
          pl.BlockSpec((tg, GM_WIDTH), lambda i: (i, 0)),
          pl.BlockSpec((tg, GM_WIDTH), lambda i: (i, 1)),
          pl.BlockSpec((1, GM_WIDTH), lambda i: (0, 0)),
          pl.BlockSpec((GM_GROUPS, CHUNK, CHUNK), lambda i: (0, 0, 0)),
          pl.BlockSpec((CHUNK, GM_WIDTH), lambda i: (0, 0)),
      ],
      out_specs=pl.BlockSpec((tg, GM_WIDTH), lambda i: (i, 0)),
      out_shape=jax.ShapeDtypeStruct((rows, GM_WIDTH), BF16),
      compiler_params=_cparams(("parallel",)),
      name="gmlp_gate",
  )(z, z, v_gain.reshape(1, GM_WIDTH), w_s, bsx)


HY_FEAT = 1 + 2 * HY_BANDS
HY_FEAT_PAD = 40
HY_HID = 64
DFT_R = 128
HY_N_LAT = 2 * SEQ
HY_N_CTX = 2 * CTX_LEN
HY_CT = 32
HY_CG = 8
assert HY_N_LAT == DFT_R * DFT_R


def _dft_consts_lat():
  n = np.arange(DFT_R)
  ang = 2.0 * np.pi * np.outer(n, n) / DFT_R
  c, s = np.cos(ang), np.sin(ang)
  tw = 2.0 * np.pi * np.outer(n, n) / HY_N_LAT
  half = SEQ // DFT_R
  fa = np.concatenate([c, -s], axis=0)
  gc = np.block([[c, -s], [s, c]])
  gci = np.block([[c, s], [-s, c]])
  fai = np.concatenate([c[:half], -s[:half]], axis=1) / HY_N_LAT
  return dict(
      fa_full=jnp.asarray(fa, BF16), fa_half=jnp.asarray(fa[:, :half], BF16),
      tw=jnp.asarray(np.stack([np.cos(tw), np.sin(tw)]), F32),
      gc=jnp.asarray(gc, BF16), gci=jnp.asarray(gci, BF16), fai=jnp.asarray(fai, BF16))


def _dft_consts_ctx():
  n = np.arange(HY_N_CTX)
  ang = 2.0 * np.pi * np.outer(n, n) / HY_N_CTX
  c, s = np.cos(ang), np.sin(ang)
  fwd = np.concatenate([c, -s], axis=1)
  inv = np.concatenate([c[:, :CTX_LEN], -s[:, :CTX_LEN]], axis=0) / HY_N_CTX
  return dict(fwd=jnp.asarray(fwd, BF16), fwd_half=jnp.asarray(fwd[:CTX_LEN], BF16),
              inv=jnp.asarray(inv, BF16))


def _hy_positions(length):
  t_norm = jnp.linspace(0.0, 1.0, length, dtype=F32)[:, None]
  pos = jnp.arange(length, dtype=F32)[:, None]
  bands = jnp.linspace(1e-4, HY_BANDS - 1, HY_BANDS, dtype=F32)[None, :]
  ang = (2.0 * math.pi / length) * pos * bands
  feats = jnp.concatenate([t_norm, jnp.cos(ang), -jnp.sin(ang)], axis=-1)
  ext = lambda a: jnp.concatenate([a, a[:1], a[:0:-1]], axis=0)
  feats_t = jnp.pad(ext(feats).T, ((0, HY_FEAT_PAD - HY_FEAT), (0, 0)))
  return feats_t, ext(t_norm).T


def _hy_mlp_kernel(ft_ref, w1_ref, w2_ref, w3_ref, pv_ref, hi_ref, lo_ref):
  hp = lax.Precision.HIGHEST
  pv = pv_ref[...]
  h = jnp.sin(pv[:, 3:4] * (jnp.dot(w1_ref[...], ft_ref[...], precision=hp,
                                    preferred_element_type=F32) + pv[:, 0:1]))
  h = jnp.sin(pv[:, 4:5] * (jnp.dot(w2_ref[...], h, precision=hp,
                                    preferred_element_type=F32) + pv[:, 1:2]))
  h = jnp.sin(pv[:, 5:6] * (jnp.dot(w3_ref[...], h, precision=hp,
                                    preferred_element_type=F32) + pv[:, 2:3]))
  hi = h.astype(BF16)
  hi_ref[...] = hi
  lo_ref[...] = (h - hi.astype(F32)).astype(BF16)


def hy_mlp(feats_t, w1, b1, w2, b2, w3, b3, freq):
  n = feats_t.shape[1]
  tl = min(n, 2048)
  w1t = jnp.pad(w1.T, ((0, 0), (0, HY_FEAT_PAD - HY_FEAT)))
  pv = jnp.stack([b1, b2, b3, freq[0], freq[1], freq[2], b1 * 0, b1 * 0], axis=1)
  full = lambda shape: pl.BlockSpec(shape, lambda i: (0, 0))
  return pl.pallas_call(
      _hy_mlp_kernel,
      grid=(n // tl,),
      in_specs=[pl.BlockSpec((HY_FEAT_PAD, tl), lambda i: (0, i)), full((HY_HID, HY_FEAT_PAD)),
                full((HY_HID, HY_HID)), full((HY_HID, HY_HID)), full((HY_HID, 8))],
      out_specs=[pl.BlockSpec((HY_HID, tl), lambda i: (0, i))] * 2,
      out_shape=[jax.ShapeDtypeStruct((HY_HID, n), BF16)] * 2,
      compiler_params=_cparams(("parallel",)),
      name="hy_mlp",
  )(feats_t, w1t, w2.T, w3.T, pv)


def _hy_filter_kernel(hi_ref, lo_ref, wfh_ref, wfl_ref, wbh_ref, wbl_ref, tn_ref, dl_ref, o_ref):
  n = hi_ref.shape[1]
  half = n // 2

  def part(wh_ref, wl_ref, sl):
    hh, hl, wh = hi_ref[:, sl], lo_ref[:, sl], wh_ref[...]
    return (jnp.dot(wh, hh, preferred_element_type=F32) + jnp.dot(wl_ref[...], hh, preferred_element_type=F32)
            + jnp.dot(wh, hl, preferred_element_type=F32))

  filt = jnp.concatenate([part(wfh_ref, wfl_ref, slice(0, half)),
                          part(wbh_ref, wbl_ref, slice(half, n))], axis=1)
  lane = lax.broadcasted_iota(jnp.int32, filt.shape, 1)
  filt = jnp.where(lane == half, 0.0, filt * jnp.exp(-tn_ref[...] * dl_ref[...]))
  o_ref[...] = filt * lax.rsqrt(jnp.sum(filt * filt, axis=-1, keepdims=True) + EPS)


def hy_filter(hdn_hi, hdn_lo, w4, tn_ext):
  n = hdn_hi.shape[1]
  c = w4.shape[1] // (2 * HY_ORDER)
  tc = 64
  w4t = w4.T.reshape(2, HY_ORDER, c, HY_HID)
  w4h = w4t.astype(BF16)
  w4l = (w4t - w4h.astype(F32)).astype(BF16)
  deltas = jnp.abs(jnp.linspace(math.log(HY_DECAY_TARGET) / HY_FAST_DECAY,
                                math.log(HY_DECAY_TARGET) / HY_SLOW_DECAY, c, dtype=F32)).reshape(c, 1)
  wspec = lambda d: pl.BlockSpec((None, None, tc, HY_HID), lambda o, i: (d, o, i, 0))
  hspec = pl.BlockSpec((HY_HID, n), lambda o, i: (0, 0))
  nct = c // tc
  out_spec = pl.BlockSpec((tc, n), lambda o, i: (o * nct + i, 0))
  out_shape = jax.ShapeDtypeStruct((HY_ORDER * c, n), F32)
  return pl.pallas_call(
      _hy_filter_kernel,
      grid=(HY_ORDER, nct),
      in_specs=[hspec, hspec, wspec(0), wspec(0), wspec(1), wspec(1),
                pl.BlockSpec((1, n), lambda o, i: (0, 0)),
                pl.BlockSpec((tc, 1), lambda o, i: (i, 0))],
      out_specs=out_spec,
      out_shape=out_shape,
      compiler_params=_cparams(("parallel", "parallel")),
      name="hy_filter",
  )(hdn_hi, hdn_lo, w4h, w4l, w4h, w4l, tn_ext, deltas)


def _short_conv_cm(x, cw, period):
  width = x.shape[1]
  assert period & (period - 1) == 0
  t = jnp.bitwise_and(lax.broadcasted_iota(jnp.int32, x.shape, 1), period - 1)
  prev = jnp.where(t == 0, 0.0, pltpu.roll(x, 1, 1))
  nxt = jnp.where(t == period - 1, 0.0, pltpu.roll(x, width - 1, 1))
  return cw[:, 3:4] + cw[:, 0:1] * prev + cw[:, 1:2] * x + cw[:, 2:3] * nxt


def _dft_stage_a(x, fa, tc, ts):
  a = jnp.dot(fa, x.astype(BF16), preferred_element_type=F32)
  ar, ai = a[:DFT_R], a[DFT_R:]
  return jnp.concatenate([ar * tc + ai * ts, ai * tc - ar * ts], axis=1).astype(BF16)


def _hyena_lat_kernel(skip_ref, g0_ref, g1_ref, z_ref, cw0_ref, cw1_ref, cwz_ref, k0_ref, k1_ref,
                      fah_ref, faf_ref, tw_ref, gc_ref, gci_ref, fai_ref, o_ref, h0_ref, h1_ref,
                      *, n_ch):
  i = pl.program_id(0)
  b = pl.program_id(1)
  tc, ts = tw_ref[0], tw_ref[1]
  rows = HY_CG * DFT_R
  half = SEQ // DFT_R
  lane = lax.broadcasted_iota(jnp.int32, (half, DFT_R), 1)
  row = lax.broadcasted_iota(jnp.int32, (half, DFT_R), 0)
  first_lane, last_lane = lane == 0, lane == DFT_R - 1
  first_row, last_row = row == 0, row == half - 1

  def short_conv(x, cw):
    back = pltpu.roll(x, 1, 1)
    wrap = jnp.where(first_row, 0.0, pltpu.roll(back, 1, 0))
    prev = jnp.where(first_lane, wrap, back)
    fwd = pltpu.roll(x, DFT_R - 1, 1)
    wrap = jnp.where(last_row, 0.0, pltpu.roll(fwd, half - 1, 0))
    nxt = jnp.where(last_lane, wrap, fwd)
    return cw[3:4] + cw[0:1] * prev + cw[1:2] * x + cw[2:3] * nxt

  @pl.when(b == 0)
  def _():
    def spectrum(g, carry):
      base = g * HY_CG
      for k_ref, h_ref in ((k0_ref, h0_ref), (k1_ref, h1_ref)):
        ap = jnp.concatenate([_dft_stage_a(k_ref[base + c], faf_ref[...], tc, ts)
                              for c in range(HY_CG)], axis=0)
        h_ref[pl.ds(pl.multiple_of(g * rows, rows), rows), :] = jnp.dot(
            ap, gc_ref[...], preferred_element_type=F32)
      return carry
    lax.fori_loop(0, HY_CT // HY_CG, spectrum, 0)

  def group(g, carry):
    base = g * HY_CG
    zs = [short_conv(z_ref[base + c], cwz_ref[base + c]) for c in range(HY_CG)]
    for o, (h_ref, gate_ref, cw_ref) in enumerate(((h0_ref, g0_ref, cw0_ref),
                                                   (h1_ref, g1_ref, cw1_ref))):
      ap = jnp.concatenate([_dft_stage_a(zs[c], fah_ref[...], tc, ts) for c in range(HY_CG)], axis=0)
      x = jnp.dot(ap, gc_ref[...], preferred_element_type=F32)
      h = h_ref[pl.ds(pl.multiple_of(g * rows, rows), rows), :]
      xr, xi, hr, hi = x[:, :DFT_R], x[:, DFT_R:], h[:, :DFT_R], h[:, DFT_R:]
      y = jnp.concatenate([xr * hr - xi * hi, xr * hi + xi * hr], axis=1).astype(BF16)
      bb = jnp.dot(y, gci_ref[...], preferred_element_type=F32)
      nxt = []
      for c in range(HY_CG):
        bc = bb[c * DFT_R:(c + 1) * DFT_R]
        br, bi = bc[:, :DFT_R], bc[:, DFT_R:]
        st = jnp.concatenate([br * tc - bi * ts, br * ts + bi * tc], axis=0).astype(BF16)
        conv = jnp.dot(fai_ref[...], st, preferred_element_type=F32)
        sk = skip_ref[o * n_ch + i * HY_CT + base + c]
        gate = short_conv(gate_ref[base + c], cw_ref[base + c])
        nxt.append(gate * (conv + sk * zs[c]))
      zs = nxt
    for c in range(HY_CG):
      o_ref[base + c] = zs[c].astype(o_ref.dtype)
    return carry
  lax.fori_loop(0, HY_CT // HY_CG, group, 0)


def hyena_lat(hy_t, cw, kern_t, skip, consts):
  c = hy_t.shape[0] // 3
  nct = c // HY_CT
  half = SEQ // DFT_R
  u3 = hy_t.reshape(3 * c, T_PAD // DFT_R, DFT_R)
  k3 = kern_t.reshape(HY_ORDER * c, DFT_R, DFT_R)
  cwx = jnp.broadcast_to(cw[:, :, None], (3 * c, 4, DFT_R))
  uspec = lambda part: pl.BlockSpec((HY_CT, half, DFT_R), lambda i, b, sk: (part * nct + i, b, 0))
  wspec = lambda part: pl.BlockSpec((HY_CT, 4, DFT_R), lambda i, b, sk: (part * nct + i, 0, 0))
  kspec = lambda o: pl.BlockSpec((HY_CT, DFT_R, DFT_R), lambda i, b, sk: (o * nct + i, 0, 0))
  cspec = lambda a: pl.BlockSpec(a.shape, lambda i, b, sk: (0,) * a.ndim)
  cs = [consts[k] for k in ("fa_half", "fa_full", "tw", "gc", "gci", "fai")]
  out = pl.pallas_call(
      functools.partial(_hyena_lat_kernel, n_ch=c),
      grid_spec=pltpu.PrefetchScalarGridSpec(
          num_scalar_prefetch=1,
          grid=(nct, BATCH),
          in_specs=[uspec(0), uspec(1), uspec(2), wspec(0), wspec(1), wspec(2), kspec(0), kspec(1)]
          + [cspec(a) for a in cs],
          out_specs=pl.BlockSpec((HY_CT, half, DFT_R), lambda i, b, sk: (i, b, 0)),
          scratch_shapes=[pltpu.VMEM((HY_CT * DFT_R, 2 * DFT_R), F32)] * 2,
      ),
      out_shape=jax.ShapeDtypeStruct((c, N_LAT // DFT_R, DFT_R), BF16),
      compiler_params=_cparams(("parallel", "arbitrary")),
      name="hyena_lat",
  )(skip.reshape(-1), u3, u3, u3, cwx, cwx, cwx, k3, k3, *cs)
  return out.reshape(c, N_LAT)


def _hyena_ctx_kernel(g0_ref, g1_ref, z_ref, cw0_ref, cw1_ref, cwz_ref, k0_ref, k1_ref, sk_ref,
                      fwd_ref, fwdh_ref, inv_ref, o_ref):
  n = HY_N_CTX
  g0 = _short_conv_cm(g0_ref[...], cw0_ref[...], CTX_LEN)
  g1 = _short_conv_cm(g1_ref[...], cw1_ref[...], CTX_LEN)
  z = _short_conv_cm(z_ref[...], cwz_ref[...], CTX_LEN)
  sk = sk_ref[...]
  spec = [jnp.dot(k_ref[...].astype(BF16), fwd_ref[...], preferred_element_type=F32)
          for k_ref in (k0_ref, k1_ref)]
  for b in range(BATCH):
    cols = slice(b * CTX_LEN, (b + 1) * CTX_LEN)
    zb = z[:, cols]
    for o, gate in enumerate((g0, g1)):
      x = jnp.dot(zb.astype(BF16), fwdh_ref[...], preferred_element_type=F32)
      h = spec[o]
      xr, xi, hr, hi = x[:, :n], x[:, n:], h[:, :n], h[:, n:]
      y = jnp.concatenate([xr * hr - xi * hi, xr * hi + xi * hr], axis=1).astype(BF16)
      conv = jnp.dot(y, inv_ref[...], preferred_element_type=F32)
      zb = gate[:, cols] * (conv + sk[:, o:o + 1] * zb)
    o_ref[:, cols] = zb.astype(o_ref.dtype)
  o_ref[:, N_CTX:] = jnp.zeros((o_ref.shape[0], TAIL_ROWS - N_CTX), o_ref.dtype)


def hyena_ctx(hy_t, cw, kern_t, skip, consts):
  c = hy_t.shape[0] // 3
  tc = 256 if c % 256 == 0 else c
  nct = c // tc
  blk = CTX_ROW0 // N_CTX
  xspec = lambda part: pl.BlockSpec((tc, N_CTX), lambda i: (part * nct + i, blk))
  wspec = lambda part: pl.BlockSpec((tc, 4), lambda i: (part * nct + i, 0))
  kspec = lambda o: pl.BlockSpec((tc, HY_N_CTX), lambda i: (o * nct + i, 0))
  cspec = lambda a: pl.BlockSpec(a.shape, lambda i: (0,) * a.ndim)
  cs = [consts[k] for k in ("fwd", "fwd_half", "inv")]
  return pl.pallas_call(
      _hyena_ctx_kernel,
      grid=(nct,),
      in_specs=[xspec(0), xspec(1), xspec(2), wspec(0), wspec(1), wspec(2), kspec(0), kspec(1),
                pl.BlockSpec((tc, HY_ORDER), lambda i: (i, 0))] + [cspec(a) for a in cs],
      out_specs=pl.BlockSpec((tc, TAIL_ROWS), lambda i: (i, 0)),
      out_shape=jax.ShapeDtypeStruct((c, TAIL_ROWS), BF16),
      compiler_params=_cparams(("parallel",)),
      name="hyena_ctx",
  )(hy_t, hy_t, hy_t, cw, cw, cw, kern_t, kern_t, skip.T, *cs)


def hyena_cm(hy_t, conv_w, conv_b, w1, b1, w2, b2, w3, b3, w4, freq, skip):
  cw = jnp.concatenate([conv_w.T, conv_b[:, None]], axis=1)
  outs = []
  for length, lat in ((SEQ, True), (CTX_LEN, False)):
    feats_t, tn_ext = _hy_positions(length)
    hdn_hi, hdn_lo = hy_mlp(feats_t, w1, b1, w2, b2, w3, b3, freq)
    kern_t = hy_filter(hdn_hi, hdn_lo, w4, tn_ext)
    if lat:
      outs.append(hyena_lat(hy_t, cw, kern_t, skip, _dft_consts_lat()))
    else:
      outs.append(hyena_ctx(hy_t, cw, kern_t, skip, _dft_consts_ctx()))
  return tuple(outs)


def _mm_nt_kernel(x_ref, w_ref, o_ref, wt_ref):
  @pl.when(pl.program_id(1) == 0)
  def _():
    wt_ref[...] = w_ref[...].T.astype(BF16)

  o_ref[...] = lax.dot_general(wt_ref[...], x_ref[...], (((1,), (1,)), ((), ())),
                               preferred_element_type=F32)


def mm_nt(x, w, layer, col0, n, tn=512):
  k = w.shape[1]
  wspec = pl.BlockSpec((None, k, tn), lambda j, i: (layer, 0, col0 // tn + j),
                       pipeline_mode=pl.Buffered(1))
  return pl.pallas_call(
      _mm_nt_kernel,
      grid=(n // tn, T_PAD // TM),
      in_specs=[_xspec(k), wspec],
      out_specs=pl.BlockSpec((tn, TM), lambda j, i: (j, i)),
      out_shape=jax.ShapeDtypeStruct((n, T_PAD), F32),
      scratch_shapes=[pltpu.VMEM((tn, k), BF16)],
      compiler_params=_cparams(("parallel", "arbitrary")),
      name="mm_nt",
  )(x, w)


def _mm_resid2_kernel(al_ref, ac_ref, zl_ref, zc_ref, wa_ref, wz_ref, s_ref, m_ref, o_ref,
                      *, n_lat_tiles):
  i = pl.program_id(1)

  def mix(a_ref, zt_ref):
    acc = jnp.dot(a_ref[...], wa_ref[...], preferred_element_type=F32)
    acc += lax.dot_general(zt_ref[...], wz_ref[...], (((0,), (0,)), ((), ())),
                           preferred_element_type=F32)
    o_ref[...] = s_ref[...] + m_ref[...] * acc

  @pl.when(i < n_lat_tiles)
  def _():
    mix(al_ref, zl_ref)

  @pl.when(i >= n_lat_tiles)
  def _():
    mix(ac_ref, zc_ref)


def mm_resid2(a_lat, a_tail, z_lat_t, z_tail_t, w, layer, s, gate8, tn=512):
  ka, kz = a_lat.shape[1], z_lat_t.shape[0]
  n = w.shape[2]
  n_lat_tiles = N_LAT // TM
  assert ka == kz and a_tail.shape[0] == TM and z_tail_t.shape[1] == TM
  row = _mod_row(TM)
  lat = lambda i: jnp.minimum(i, n_lat_tiles - 1)
  once = pl.Buffered(1)
  return pl.pallas_call(
      functools.partial(_mm_resid2_kernel, n_lat_tiles=n_lat_tiles),
      grid=(n // tn, T_PAD // TM),
      in_specs=[
          pl.BlockSpec((TM, ka), lambda j, i: (lat(i), 0)),
          pl.BlockSpec((TM, ka), lambda j, i: (0, 0), pipeline_mode=once),
          pl.BlockSpec((kz, TM), lambda j, i: (0, lat(i))),
          pl.BlockSpec((kz, TM), lambda j, i: (0, 0), pipeline_mode=once),
          pl.BlockSpec((None, ka, tn), lambda j, i: (layer, 0, j)),
          pl.BlockSpec((None, kz, tn), lambda j, i: (layer, 1, j)),
          pl.BlockSpec((TM, tn), lambda j, i: (i, j)),
          pl.BlockSpec((None, 1, tn), lambda j, i: (row(i), 0, j)),
      ],
      out_specs=pl.BlockSpec((TM, tn), lambda j, i: (i, j)),
      out_shape=jax.ShapeDtypeStruct((T_PAD, n), F32),
      input_output_aliases={6: 0},
      compiler_params=_cparams(("parallel", "parallel")),
      name="mm_resid2",
  )(a_lat, a_tail, z_lat_t, z_tail_t, w, w, s, gate8)


def kernel(x, c, ctx, c_ctx, ada_a, ada_b, ada_bias, norm_mix, norm_ffn, even_w_in, na_q_gain,
           na_k_gain, na_rpb, hy_conv_w, hy_conv_b, hy_w1, hy_b1, hy_w2, hy_b2, hy_w3, hy_b3,
           hy_w4, hy_freq, hy_skip, odd_w_in, odd_b_in, gm_v_gain, gm_w_s, gm_b_s, w_mix_out,
           moe_router, moe_bias, moe_w_gate, moe_w_up, moe_w_down, sh_w_gate, sh_w_up, sh_w_down):
  s = jnp.concatenate([x.reshape(N_LAT, D_MODEL), ctx.reshape(N_CTX, D_MODEL),
                       jnp.zeros((T_PAD - T_REAL, D_MODEL), F32)], axis=0)
  cond8 = jnp.concatenate([c, c_ctx[None], jnp.zeros((8 - BATCH - 1, D_MODEL), F32)], axis=0)
  mods = ada_table(cond8, ada_a, ada_b, ada_bias).reshape(DEPTH, 8, N_MOD, 1, D_MODEL)
  w_qkv = even_w_in[:, :, :3 * NA_WIDTH].astype(BF16)
  w_odd = odd_w_in.astype(BF16)
  w_out = w_mix_out.astype(BF16)
  wsg, wsu = sh_w_gate.astype(BF16), sh_w_up.astype(BF16)
  wd_r = moe_w_down.reshape(DEPTH, N_EXPERTS * EXPERT_DIM, D_MODEL).astype(BF16)
  wd_s = sh_w_down.astype(BF16)

  for layer in range(DEPTH):
    j = layer // 2
    rows = N_LAT if layer == DEPTH - 1 else T_PAD
    mod = [mods[layer, :, i] for i in range(N_MOD)]
    if layer % 2 == 0:
      h = norm_mod(s, norm_mix[layer], mod[0], mod[1])
      gains = jnp.concatenate([jnp.tile(na_q_gain[j], NA_HEADS), jnp.tile(na_k_gain[j], NA_HEADS),
                               jnp.ones((NA_WIDTH,), F32)]).reshape(1, 3 * NA_WIDTH)
      qkv = mm_headnorm(h, w_qkv, j, gains, 3 * NA_WIDTH, 2 * NA_WIDTH)
      hy_t = mm_nt(h, even_w_in, j, 3 * NA_WIDTH, 3 * HY_WIDTH)
      a_lat = na_attention(qkv, na_bias(na_rpb[j]))
      a_tail = ctx_attention(qkv)
      z_lat_t, z_tail_t = hyena_cm(hy_t, hy_conv_w[j], hy_conv_b[j], hy_w1[j], hy_b1[j], hy_w2[j],
                                   hy_b2[j], hy_w3[j], hy_b3[j], hy_w4[j], hy_freq[j], hy_skip[j])
      s = mm_resid2(a_lat, a_tail, z_lat_t, z_tail_t, w_out, layer, s, mod[2])
    else:
      h = norm_mod(s, norm_mix[layer], mod[0], mod[1], rows)
      z = mm_gelu(h, w_odd, j, odd_b_in[j], rows)
      bsx = jnp.repeat(gm_b_s[j].T, HEAD_DIM, axis=1)
      mix = gmlp_gate(z, gm_v_gain[j], gm_w_s[j].astype(BF16), bsx)
      s = mm_resid(mix, w_out, layer, s, mod[2], rows)

    f, gates = norm_mod_router(s, norm_ffn[layer], mod[3], mod[4], moe_router[layer], moe_bias[layer],
                               rows)
    gates_t = gates[:, :N_EXPERTS].reshape(rows, N_EXPERTS // EXP_PER_TILE, EXP_PER_TILE)
    gates_t = gates_t.transpose(1, 0, 2)
    hid_r = moe_up(f, moe_w_gate, moe_w_up, layer, gates_t, rows)
    hid_s = moe_up_shared(f, wsg, wsu, layer, rows)
    s = moe_down(hid_r, hid_s, wd_r, wd_s, layer, s, mod[5], rows)

  return s.reshape(BATCH, SEQ, D_MODEL)
```

```python
import functools
import math

import numpy as np
import jax
import jax.numpy as jnp
from jax import lax
from jax.experimental import pallas as pl
from jax.experimental.pallas import tpu as pltpu

F32 = jnp.float32
BF16 = jnp.bfloat16

D_MODEL = 4096
BATCH = 2
SEQ = 8192
DEPTH = 4
GRID_W = 64
ROWS = SEQ // GRID_W
CTX_LEN = 256
HEAD_DIM = 128
NA_HEADS = 16
NA_WIDTH = NA_HEADS * HEAD_DIM
HY_WIDTH = D_MODEL - NA_WIDTH
WIN_H = 8
WIN_W = 16
HY_ORDER = 2
HY_SHORT = 3
HY_BANDS = 16
HY_DECAY_TARGET = 1e-2
HY_FAST_DECAY = 0.3
HY_SLOW_DECAY = 1.5
GM_WIDTH = D_MODEL
GM_GROUPS = GM_WIDTH // HEAD_DIM
CHUNK = 128
N_EXPERTS = 64
TOP_K = 8
EXPERT_DIM = 128
SHARED_DIM = TOP_K * EXPERT_DIM
ROUTED_SCALE = 2.5
N_MOD = 6
EPS = 1e-6

LANES = 128
VMEM_LIMIT = 56 * 1024 * 1024

TM = 1024
N_LAT = BATCH * SEQ
N_CTX = BATCH * CTX_LEN
T_REAL = N_LAT + N_CTX
T_PAD = -(-T_REAL // TM) * TM
CTX_ROW0 = N_LAT
TAIL_ROWS = T_PAD - N_LAT
assert TAIL_ROWS == TM
TN_NORM = 512
MOE_TN = 512
EXP_PER_TILE = MOE_TN // EXPERT_DIM
NEG = -1e30

NA_R = 4
NA_Q = NA_R * GRID_W
NA_KROWS = NA_R + WIN_H
NA_KTOK = NA_KROWS * GRID_W
NA_QBLOCKS = ROWS // NA_R
NA_HP = 2


def _mod_row(tile_rows):
  def f(i):
    return jnp.minimum((i * tile_rows) // SEQ, BATCH)
  return f


def _cparams(sem):
  return pltpu.CompilerParams(dimension_semantics=sem, vmem_limit_bytes=VMEM_LIMIT)


def _ada_kernel(cond_ref, a_ref, b_ref, bias_ref, o_ref):
  cnd = cond_ref[...]
  sc = cnd * jax.nn.sigmoid(cnd)
  t = jnp.dot(sc, a_ref[...], preferred_element_type=F32, precision=lax.Precision.HIGHEST)
  o_ref[...] = jnp.dot(t, b_ref[...], preferred_element_type=F32,
                       precision=lax.Precision.HIGHEST) + bias_ref[...]


def ada_table(cond8, ada_a, ada_b, ada_bias):
  tn = 4096
  rank = ada_a.shape[-1]
  nmd = N_MOD * D_MODEL
  return pl.pallas_call(
      _ada_kernel,
      grid=(DEPTH, nmd // tn),
      in_specs=[
          pl.BlockSpec((8, D_MODEL), lambda l, j: (0, 0)),
          pl.BlockSpec((None, D_MODEL, rank), lambda l, j: (l, 0, 0)),
          pl.BlockSpec((None, rank, tn), lambda l, j: (l, 0, j)),
          pl.BlockSpec((None, 1, tn), lambda l, j: (l, 0, j)),
      ],
      out_specs=pl.BlockSpec((None, 8, tn), lambda l, j: (l, 0, j)),
      out_shape=jax.ShapeDtypeStruct((DEPTH, 8, nmd), F32),
      compiler_params=_cparams(("arbitrary", "arbitrary")),
      name="ada_table",
  )(cond8, ada_a, ada_b, ada_bias.reshape(DEPTH, 1, nmd))


def _norm_mod(x, g, shift, scale):
  ms = jnp.mean(x * x, axis=-1, keepdims=True)
  y = x * lax.rsqrt(ms + EPS) * g
  return y * (1.0 + scale) + shift


def _norm_kernel(x_ref, g_ref, shift_ref, scale_ref, o_ref):
  o_ref[...] = _norm_mod(x_ref[...], g_ref[...], shift_ref[...], scale_ref[...]).astype(BF16)


def _norm_router_kernel(x_ref, g_ref, shift_ref, scale_ref, w_ref, rb_ref, o_ref, gates_ref):
  f = _norm_mod(x_ref[...], g_ref[...], shift_ref[...], scale_ref[...])
  f_hi = f.astype(BF16)
  o_ref[...] = f_hi
  f_lo = (f - f_hi.astype(F32)).astype(BF16)
  hi = jnp.dot(f_hi, w_ref[...], preferred_element_type=F32)
  lo = jnp.dot(f_lo, w_ref[:, :LANES], preferred_element_type=F32)
  logits = hi[:, :LANES] + lo + hi[:, LANES:]
  scores = jax.nn.sigmoid(logits)
  lane = lax.broadcasted_iota(jnp.int32, scores.shape, 1).astype(F32)
  work = jnp.where(lane < N_EXPERTS, scores + rb_ref[...], NEG)
  sel = jnp.zeros(scores.shape, F32)
  for _ in range(TOP_K):
    m = jnp.max(work, axis=-1, keepdims=True)
    first = jnp.min(jnp.where(work == m, lane, float(LANES)), axis=-1, keepdims=True)
    hit = lane == first
    sel = jnp.where(hit, scores, sel)
    work = jnp.where(hit, 3.0 * NEG, work)
  gates_ref[...] = sel / jnp.sum(sel, axis=-1, keepdims=True) * ROUTED_SCALE


def norm_mod(s, gain, shift8, scale8, rows=T_PAD):
  n = rows // TN_NORM
  row = _mod_row(TN_NORM)
  return pl.pallas_call(
      _norm_kernel,
      grid=(n,),
      in_specs=[
          pl.BlockSpec((TN_NORM, D_MODEL), lambda i: (i, 0)),
          pl.BlockSpec((1, D_MODEL), lambda i: (0, 0)),
          pl.BlockSpec((None, 1, D_MODEL), lambda i: (row(i), 0, 0)),
          pl.BlockSpec((None, 1, D_MODEL), lambda i: (row(i), 0, 0)),
      ],
      out_specs=pl.BlockSpec((TN_NORM, D_MODEL), lambda i: (i, 0)),
      out_shape=jax.ShapeDtypeStruct((rows, D_MODEL), BF16),
      compiler_params=_cparams(("parallel",)),
      name="norm_mod",
  )(s, gain.reshape(1, D_MODEL), shift8, scale8)


def norm_mod_router(s, gain, shift8, scale8, w_router, r_bias, rows=T_PAD):
  n = rows // TN_NORM
  row = _mod_row(TN_NORM)
  w_pad = jnp.pad(w_router, ((0, 0), (0, LANES - N_EXPERTS)))
  w_hi = w_pad.astype(BF16)
  w_lo = (w_pad - w_hi.astype(F32)).astype(BF16)
  w_cat = jnp.concatenate([w_hi, w_lo], axis=1)
  rb = jnp.pad(r_bias.astype(F32), (0, LANES - N_EXPERTS)).reshape(1, LANES)
  return pl.pallas_call(
      _norm_router_kernel,
      grid=(n,),
      in_specs=[
          pl.BlockSpec((TN_NORM, D_MODEL), lambda i: (i, 0)),
          pl.BlockSpec((1, D_MODEL), lambda i: (0, 0)),
          pl.BlockSpec((None, 1, D_MODEL), lambda i: (row(i), 0, 0)),
          pl.BlockSpec((None, 1, D_MODEL), lambda i: (row(i), 0, 0)),
          pl.BlockSpec((D_MODEL, 2 * LANES), lambda i: (0, 0)),
          pl.BlockSpec((1, LANES), lambda i: (0, 0)),
      ],
      out_specs=[
          pl.BlockSpec((TN_NORM, D_MODEL), lambda i: (i, 0)),
          pl.BlockSpec((TN_NORM, LANES), lambda i: (i, 0)),
      ],
      out_shape=[
          jax.ShapeDtypeStruct((rows, D_MODEL), BF16),
          jax.ShapeDtypeStruct((rows, LANES), F32),
      ],
      compiler_params=_cparams(("parallel",)),
      name="norm_mod_router",
  )(s, gain.reshape(1, D_MODEL), shift8, scale8, w_cat, rb)


def _gelu_exact(x):
  return 0.5 * x * (1.0 + lax.erf(x * (1.0 / math.sqrt(2.0))))


MXU_N = 256


def _col_chunks(width):
  return [slice(c, c + MXU_N) for c in range(0, width, MXU_N)]


def _mm_headnorm_kernel(x_ref, w_ref, g_ref, o_ref, *, n_norm_tiles):
  j = pl.program_id(0)
  x = x_ref[...]

  @pl.when(j < n_norm_tiles)
  def _():
    for cs in _col_chunks(o_ref.shape[1]):
      acc = jnp.dot(x, w_ref[:, cs], preferred_element_type=F32)
      for h in range(MXU_N // HEAD_DIM):
        blk = acc[:, h * HEAD_DIM:(h + 1) * HEAD_DIM]
        hs = slice(cs.start + h * HEAD_DIM, cs.start + (h + 1) * HEAD_DIM)
        ms = jnp.mean(blk * blk, axis=-1, keepdims=True)
        o_ref[:, hs] = (blk * lax.rsqrt(ms + EPS) * g_ref[:, hs]).astype(o_ref.dtype)

  @pl.when(j >= n_norm_tiles)
  def _():
    o_ref[...] = jnp.dot(x, w_ref[...], preferred_element_type=F32).astype(o_ref.dtype)


def _mm_gelu_kernel(x_ref, w_ref, b_ref, o_ref):
  x = x_ref[...]
  for cs in _col_chunks(o_ref.shape[1]):
    acc = jnp.dot(x, w_ref[:, cs], preferred_element_type=F32) + b_ref[:, cs]
    o_ref[:, cs] = _gelu_exact(acc).astype(o_ref.dtype)


def _mm_resid_kernel(x_ref, w_ref, s_ref, m_ref, o_ref):
  acc = jnp.dot(x_ref[...], w_ref[...], preferred_element_type=F32)
  o_ref[...] = s_ref[...] + m_ref[...] * acc


def _moe_down_kernel(hr_ref, hs_ref, wr_ref, ws_ref, s_ref, m_ref, o_ref, acc_ref):
  k = pl.program_id(2)

  @pl.when(k == 0)
  def _():
    acc_ref[...] = jnp.zeros_like(acc_ref)

  acc_ref[...] += jnp.dot(hr_ref[...], wr_ref[...], preferred_element_type=F32)

  @pl.when(k == pl.num_programs(2) - 1)
  def _():
    acc = acc_ref[...] + jnp.dot(hs_ref[...], ws_ref[...], preferred_element_type=F32)
    o_ref[...] = s_ref[...] + m_ref[...] * acc


def _swiglu(x, wg, wu):
  g = jnp.dot(x, wg, preferred_element_type=F32)
  u = jnp.dot(x, wu, preferred_element_type=F32)
  return g * jax.nn.sigmoid(g) * u


def _moe_up_kernel(x_ref, wg_ref, wu_ref, gate_ref, o_ref, wgb_ref, wub_ref):
  @pl.when(pl.program_id(1) == 0)
  def _():
    for e in range(EXP_PER_TILE):
      sl = slice(e * EXPERT_DIM, (e + 1) * EXPERT_DIM)
      wgb_ref[:, sl] = wg_ref[e].astype(BF16)
      wub_ref[:, sl] = wu_ref[e].astype(BF16)

  x = x_ref[...]
  gate = gate_ref[...]
  for cs in _col_chunks(o_ref.shape[1]):
    hid = _swiglu(x, wgb_ref[:, cs], wub_ref[:, cs])
    for e in range(MXU_N // EXPERT_DIM):
      col = cs.start + e * EXPERT_DIM
      g = gate[:, col // EXPERT_DIM:col // EXPERT_DIM + 1]
      o_ref[:, col:col + EXPERT_DIM] = (hid[:, e * EXPERT_DIM:(e + 1) * EXPERT_DIM] * g).astype(o_ref.dtype)


def _moe_up_shared_kernel(x_ref, wg_ref, wu_ref, o_ref):
  x = x_ref[...]
  for cs in _col_chunks(o_ref.shape[1]):
    o_ref[:, cs] = _swiglu(x, wg_ref[:, cs], wu_ref[:, cs]).astype(o_ref.dtype)


def _xspec(k):
  return pl.BlockSpec((TM, k), lambda j, i: (i, 0))


def _wspec(k, tn, layer, col0=0):
  return pl.BlockSpec((None, k, tn), lambda j, i: (layer, 0, col0 // tn + j))


def mm_headnorm(x, w, layer, gains, n, n_norm_cols, tn=1024):
  k = w.shape[1]
  return pl.pallas_call(
      functools.partial(_mm_headnorm_kernel, n_norm_tiles=n_norm_cols // tn),
      grid=(n // tn, T_PAD // TM),
      in_specs=[_xspec(k), _wspec(k, tn, layer), pl.BlockSpec((1, tn), lambda j, i: (0, j))],
      out_specs=pl.BlockSpec((TM, tn), lambda j, i: (i, j)),
      out_shape=jax.ShapeDtypeStruct((T_PAD, n), BF16),
      compiler_params=_cparams(("parallel", "parallel")),
      name="mm_headnorm",
  )(x, w, gains)


def mm_gelu(x, w, layer, b, rows=T_PAD, tn=1024):
  _, k, n = w.shape
  return pl.pallas_call(
      _mm_gelu_kernel,
      grid=(n // tn, rows // TM),
      in_specs=[_xspec(k), _wspec(k, tn, layer), pl.BlockSpec((1, tn), lambda j, i: (0, j))],
      out_specs=pl.BlockSpec((TM, tn), lambda j, i: (i, j)),
      out_shape=jax.ShapeDtypeStruct((rows, n), BF16),
      compiler_params=_cparams(("parallel", "parallel")),
      name="mm_gelu",
  )(x, w, b.reshape(1, n))


def mm_resid(x, w, layer, s, gate8, rows=T_PAD, tn=512):
  _, k, n = w.shape
  row = _mod_row(TM)
  return pl.pallas_call(
      _mm_resid_kernel,
      grid=(n // tn, rows // TM),
      in_specs=[_xspec(k), _wspec(k, tn, layer),
                pl.BlockSpec((TM, tn), lambda j, i: (i, j)),
                pl.BlockSpec((None, 1, tn), lambda j, i: (row(i), 0, j))],
      out_specs=pl.BlockSpec((TM, tn), lambda j, i: (i, j)),
      out_shape=jax.ShapeDtypeStruct(s.shape, F32),
      input_output_aliases={2: 0},
      compiler_params=_cparams(("parallel", "parallel")),
      name="mm_resid",
  )(x, w, s, gate8)


def moe_down(hid_r, hid_s, wd_r, wd_s, layer, s, gate8, out_rows=T_PAD, tn=1024, tk=2048):
  _, kr, n = wd_r.shape
  ks = wd_s.shape[1]
  row = _mod_row(TM)
  return pl.pallas_call(
      _moe_down_kernel,
      grid=(n // tn, out_rows // TM, kr // tk),
      in_specs=[
          pl.BlockSpec((TM, tk), lambda j, i, kk: (i, kk)),
          pl.BlockSpec((TM, ks), lambda j, i, kk: (i, 0)),
          pl.BlockSpec((None, tk, tn), lambda j, i, kk: (layer, kk, j)),
          pl.BlockSpec((None, ks, tn), lambda j, i, kk: (layer, 0, j)),
          pl.BlockSpec((TM, tn), lambda j, i, kk: (i, j)),
          pl.BlockSpec((None, 1, tn), lambda j, i, kk: (row(i), 0, j)),
      ],
      out_specs=pl.BlockSpec((TM, tn), lambda j, i, kk: (i, j)),
      out_shape=jax.ShapeDtypeStruct((out_rows, n), F32),
      scratch_shapes=[pltpu.VMEM((TM, tn), F32)],
      input_output_aliases={4: 0} if out_rows == T_PAD else {},
      compiler_params=_cparams(("parallel", "parallel", "arbitrary")),
      name="moe_down",
  )(hid_r, hid_s, wd_r, wd_s, s, gate8)


def moe_up(x, w_gate, w_up, layer, gates_t, rows=T_PAD):
  _, n_exp, k, f = w_gate.shape
  tn = MOE_TN
  wspec = pl.BlockSpec((None, EXP_PER_TILE, k, f), lambda j, i: (layer, j, 0, 0),
                       pipeline_mode=pl.Buffered(1))
  return pl.pallas_call(
      _moe_up_kernel,
      grid=(n_exp // EXP_PER_TILE, rows // TM),
      in_specs=[_xspec(k), wspec, wspec,
                pl.BlockSpec((None, TM, EXP_PER_TILE), lambda j, i: (j, i, 0))],
      out_specs=pl.BlockSpec((TM, tn), lambda j, i: (i, j)),
      out_shape=jax.ShapeDtypeStruct((rows, n_exp * f), BF16),
      scratch_shapes=[pltpu.VMEM((k, tn), BF16)] * 2,
      compiler_params=_cparams(("parallel", "arbitrary")),
      name="moe_up",
  )(x, w_gate, w_up, gates_t)


def moe_up_shared(x, wg, wu, layer, rows=T_PAD):
  _, k, n = wg.shape
  tn = MOE_TN
  return pl.pallas_call(
      _moe_up_shared_kernel,
      grid=(n // tn, rows // TM),
      in_specs=[_xspec(k), _wspec(k, tn, layer), _wspec(k, tn, layer)],
      out_specs=pl.BlockSpec((TM, tn), lambda j, i: (i, j)),
      out_shape=jax.ShapeDtypeStruct((rows, n), BF16),
      compiler_params=_cparams(("parallel", "parallel")),
      name="moe_up_shared",
  )(x, wg, wu)


def _na_kernel(q_ref, k_ref, v_ref, kc_ref, vc_ref, bias_ref, o_ref):
  qb = pl.program_id(2)
  kb = jnp.clip(qb * NA_R - WIN_H // 2, 0, ROWS - NA_KROWS)
  start = pl.multiple_of(kb * GRID_W, GRID_W)
  kind = jnp.where(qb == 0, 0, jnp.where(qb == NA_QBLOCKS - 1, 2, 1))
  scale = HEAD_DIM ** -0.5
  nt = (((1,), (1,)), ((), ()))
  for hh in range(NA_HP):
    cols = slice(hh * HEAD_DIM, (hh + 1) * HEAD_DIM)
    q = q_ref[:, cols]
    kw = k_ref[pl.ds(start, NA_KTOK), cols]
    vw = v_ref[pl.ds(start, NA_KTOK), cols]
    s = lax.dot_general(q, kw, nt, preferred_element_type=F32) * scale + bias_ref[kind, hh]
    sc = lax.dot_general(q, kc_ref[:, cols], nt, preferred_element_type=F32) * scale
    m = jnp.maximum(jnp.max(s, axis=-1, keepdims=True), jnp.max(sc, axis=-1, keepdims=True))
    p = jnp.exp(s - m)
    pc = jnp.exp(sc - m)
    den = jnp.sum(p, axis=-1, keepdims=True) + jnp.sum(pc, axis=-1, keepdims=True)
    o = (jnp.dot(p.astype(BF16), vw, preferred_element_type=F32)
         + jnp.dot(pc.astype(BF16), vc_ref[:, cols], preferred_element_type=F32))
    o_ref[:, cols] = (o / den).astype(o_ref.dtype)


def _na_bias_selectors():
  rows = np.zeros((3, NA_R, NA_KROWS, 2 * WIN_H - 1), np.float32)
  for kind, qb in enumerate((0, 1, NA_QBLOCKS - 1)):
    r0 = qb * NA_R
    kb = int(np.clip(r0 - WIN_H // 2, 0, ROWS - NA_KROWS))
    for i in range(NA_R):
      r = r0 + i
      rs = int(np.clip(r - WIN_H // 2, 0, ROWS - WIN_H))
      for j in range(NA_KROWS):
        if rs <= kb + j < rs + WIN_H:
          rows[kind, i, j, kb + j - r + WIN_H - 1] = 1.0
  cols = np.zeros((2 * WIN_W - 1, GRID_W, GRID_W), np.float32)
  for c in range(GRID_W):
    cs = int(np.clip(c - WIN_W // 2, 0, GRID_W - WIN_W))
    for d in range(cs, cs + WIN_W):
      cols[d - c + WIN_W - 1, c, d] = 1.0
  return rows, cols


def na_bias(rpb):
  rows, cols = _na_bias_selectors()
  hp = lax.Precision.HIGHEST
  colband = jnp.einsum('hrw,wcd->hrcd', rpb.astype(F32), cols, precision=hp)
  band = jnp.einsum('tijr,hrcd->thicjd', rows, colband, precision=hp)
  valid = np.einsum('tijr,wcd->ticjd', rows, cols) > 0.5
  band = jnp.where(valid[:, None], band, NEG)
  return band.reshape(3, NA_HEADS, NA_Q, NA_KTOK)


def na_attention(qkv, bias):
  hq = NA_HEADS // NA_HP
  width = NA_HP * HEAD_DIM
  lat_blocks = SEQ // NA_Q
  ctx_blk0 = CTX_ROW0 // CTX_LEN
  return pl.pallas_call(
      _na_kernel,
      grid=(BATCH, hq, NA_QBLOCKS),
      in_specs=[
          pl.BlockSpec((NA_Q, width), lambda b, h, i: (b * lat_blocks + i, h)),
          pl.BlockSpec((SEQ, width), lambda b, h, i: (b, hq + h)),
          pl.BlockSpec((SEQ, width), lambda b, h, i: (b, 2 * hq + h)),
          pl.BlockSpec((CTX_LEN, width), lambda b, h, i: (ctx_blk0 + b, hq + h)),
          pl.BlockSpec((CTX_LEN, width), lambda b, h, i: (ctx_blk0 + b, 2 * hq + h)),
          pl.BlockSpec((3, NA_HP, NA_Q, NA_KTOK), lambda b, h, i: (0, h, 0, 0)),
      ],
      out_specs=pl.BlockSpec((NA_Q, width), lambda b, h, i: (b * lat_blocks + i, h)),
      out_shape=jax.ShapeDtypeStruct((N_LAT, NA_WIDTH), BF16),
      compiler_params=_cparams(("parallel", "parallel", "arbitrary")),
      name="na_attention",
  )(qkv, qkv, qkv, qkv, qkv, bias)


def _ctx_attn_kernel(q_ref, k_ref, v_ref, o_ref):
  @pl.when(pl.program_id(0) < BATCH)
  def _():
    scale = HEAD_DIM ** -0.5
    s = lax.dot_general(q_ref[...], k_ref[...], (((1,), (1,)), ((), ())),
                        preferred_element_type=F32) * scale
    m = jnp.max(s, axis=-1, keepdims=True)
    p = jnp.exp(s - m)
    den = jnp.sum(p, axis=-1, keepdims=True)
    o = jnp.dot(p.astype(BF16), v_ref[...], preferred_element_type=F32)
    o_ref[...] = (o / den).astype(o_ref.dtype)

  @pl.when(pl.program_id(0) >= BATCH)
  def _():
    o_ref[...] = jnp.zeros_like(o_ref)


def ctx_attention(qkv):
  hq = NA_HEADS
  ctx_blk0 = CTX_ROW0 // CTX_LEN
  blk = lambda b: ctx_blk0 + jnp.minimum(b, BATCH - 1)
  return pl.pallas_call(
      _ctx_attn_kernel,
      grid=(TAIL_ROWS // CTX_LEN, NA_HEADS),
      in_specs=[
          pl.BlockSpec((CTX_LEN, HEAD_DIM), lambda b, h: (blk(b), h)),
          pl.BlockSpec((CTX_LEN, HEAD_DIM), lambda b, h: (blk(b), hq + h)),
          pl.BlockSpec((CTX_LEN, HEAD_DIM), lambda b, h: (blk(b), 2 * hq + h)),
      ],
      out_specs=pl.BlockSpec((CTX_LEN, HEAD_DIM), lambda b, h: (b, h)),
      out_shape=jax.ShapeDtypeStruct((TAIL_ROWS, NA_WIDTH), BF16),
      compiler_params=_cparams(("parallel", "parallel")),
      name="ctx_attention",
  )(qkv, qkv, qkv)


def _gmlp_gate_kernel(u_ref, v_ref, vg_ref, ws_ref, bs_ref, o_ref):
  v = v_ref[...].astype(F32)
  ms = jnp.mean(v * v, axis=-1, keepdims=True)
  vn = (v * lax.rsqrt(ms + EPS) * vg_ref[...]).astype(BF16)
  for c in range(v.shape[0] // CHUNK):
    rows = slice(c * CHUNK, (c + 1) * CHUNK)
    for g in range(GM_GROUPS):
      cols = slice(g * HEAD_DIM, (g + 1) * HEAD_DIM)
      sp = jnp.dot(ws_ref[g], vn[rows, cols], preferred_element_type=F32) + bs_ref[:, cols]
      o_ref[rows, cols] = (u_ref[rows, cols].astype(F32) * sp).astype(o_ref.dtype)


def gmlp_gate(z, v_gain, w_s, bsx):
  tg = 256
  rows = z.shape[0]
  return pl.pallas_call(
      _gmlp_gate_kernel,
      grid=(rows // tg,),
      in_specs=[
          pl.BlockSpec((tg, GM_WIDTH), lambda i: (i, 0)),
          pl.BlockSpec((tg, GM_WIDTH), lambda i: (i, 1)),
          pl.BlockSpec((1, GM_WIDTH), lambda i: (0, 0)),
          pl.BlockSpec((GM_GROUPS, CHUNK, CHUNK), lambda i: (0, 0, 0)),
          pl.BlockSpec((CHUNK, GM_WIDTH), lambda i: (0, 0)),
      ],
      out_specs=pl.BlockSpec((tg, GM_WIDTH), lambda i: (i, 0)),
      out_shape=jax.ShapeDtypeStruct((rows, GM_WIDTH), BF16),
      compiler_params=_cparams(("parallel",)),
      name="gmlp_gate",
  )(z, z, v_gain.reshape(1, GM_WIDTH), w_s, bsx)


HY_FEAT = 1 + 2 * HY_BANDS
HY_FEAT_PAD = 40
HY_HID = 64
DFT_R = 128
HY_N_LAT = 2 * SEQ
HY_N_CTX = 2 * CTX_LEN
HY_CT = 32
HY_CG = 8
assert HY_N_LAT == DFT_R * DFT_R


def _dft_consts_lat():
  n = np.arange(DFT_R)
  ang = 2.0 * np.pi * np.outer(n, n) / DFT_R
  c, s = np.cos(ang), np.sin(ang)
  tw = 2.0 * np.pi * np.outer(n, n) / HY_N_LAT
  half = SEQ // DFT_R
  fa = np.concatenate([c, -s], axis=0)
  gc = np.block([[c, -s], [s, c]])
  gci = np.block([[c, s], [-s, c]])
  fai = np.concatenate([c[:half], -s[:half]], axis=1) / HY_N_LAT
  return dict(
      fa_full=jnp.asarray(fa, BF16), fa_half=jnp.asarray(fa[:, :half], BF16),
      tw=jnp.asarray(np.stack([np.cos(tw), np.sin(tw)]), F32),
      gc=jnp.asarray(gc, BF16), gci=jnp.asarray(gci, BF16), fai=jnp.asarray(fai, BF16))


def _dft_consts_ctx():
  n = np.arange(HY_N_CTX)
  ang = 2.0 * np.pi * np.outer(n, n) / HY_N_CTX
  c, s = np.cos(ang), np.sin(ang)
  fwd = np.concatenate([c, -s], axis=1)
  inv = np.concatenate([c[:, :CTX_LEN], -s[:, :CTX_LEN]], axis=0) / HY_N_CTX
  return dict(fwd=jnp.asarray(fwd, BF16), fwd_half=jnp.asarray(fwd[:CTX_LEN], BF16),
              inv=jnp.asarray(inv, BF16))


def _hy_positions(length):
  t_norm = jnp.linspace(0.0, 1.0, length, dtype=F32)[:, None]
  pos = jnp.arange(length, dtype=F32)[:, None]
  bands = jnp.linspace(1e-4, HY_BANDS - 1, HY_BANDS, dtype=F32)[None, :]
  ang = (2.0 * math.pi / length) * pos * bands
  feats = jnp.concatenate([t_norm, jnp.cos(ang), -jnp.sin(ang)], axis=-1)
  ext = lambda a: jnp.concatenate([a, a[:1], a[:0:-1]], axis=0)
  feats_t = jnp.pad(ext(feats).T, ((0, HY_FEAT_PAD - HY_FEAT), (0, 0)))
  return feats_t, ext(t_norm).T


def _hy_mlp_kernel(ft_ref, w1_ref, w2_ref, w3_ref, pv_ref, hi_ref, lo_ref):
  hp = lax.Precision.HIGHEST
  pv = pv_ref[...]
  h = jnp.sin(pv[:, 3:4] * (jnp.dot(w1_ref[...], ft_ref[...], precision=hp,
                                    preferred_element_type=F32) + pv[:, 0:1]))
  h = jnp.sin(pv[:, 4:5] * (jnp.dot(w2_ref[...], h, precision=hp,
                                    preferred_element_type=F32) + pv[:, 1:2]))
  h = jnp.sin(pv[:, 5:6] * (jnp.dot(w3_ref[...], h, precision=hp,
                                    preferred_element_type=F32) + pv[:, 2:3]))
  hi = h.astype(BF16)
  hi_ref[...] = hi
  lo_ref[...] = (h - hi.astype(F32)).astype(BF16)


def hy_mlp(feats_t, w1, b1, w2, b2, w3, b3, freq):
  n = feats_t.shape[1]
  tl = min(n, 2048)
  w1t = jnp.pad(w1.T, ((0, 0), (0, HY_FEAT_PAD - HY_FEAT)))
  pv = jnp.stack([b1, b2, b3, freq[0], freq[1], freq[2], b1 * 0, b1 * 0], axis=1)
  full = lambda shape: pl.BlockSpec(shape, lambda i: (0, 0))
  return pl.pallas_call(
      _hy_mlp_kernel,
      grid=(n // tl,),
      in_specs=[pl.BlockSpec((HY_FEAT_PAD, tl), lambda i: (0, i)), full((HY_HID, HY_FEAT_PAD)),
                full((HY_HID, HY_HID)), full((HY_HID, HY_HID)), full((HY_HID, 8))],
      out_specs=[pl.BlockSpec((HY_HID, tl), lambda i: (0, i))] * 2,
      out_shape=[jax.ShapeDtypeStruct((HY_HID, n), BF16)] * 2,
      compiler_params=_cparams(("parallel",)),
      name="hy_mlp",
  )(feats_t, w1t, w2.T, w3.T, pv)


def _hy_filter_kernel(hi_ref, lo_ref, wfh_ref, wfl_ref, wbh_ref, wbl_ref, tn_ref, dl_ref, o_ref):
  n = hi_ref.shape[1]
  half = n // 2

  def part(wh_ref, wl_ref, sl):
    hh, hl, wh = hi_ref[:, sl], lo_ref[:, sl], wh_ref[...]
    return (jnp.dot(wh, hh, preferred_element_type=F32) + jnp.dot(wl_ref[...], hh, preferred_element_type=F32)
            + jnp.dot(wh, hl, preferred_element_type=F32))

  filt = jnp.concatenate([part(wfh_ref, wfl_ref, slice(0, half)),
                          part(wbh_ref, wbl_ref, slice(half, n))], axis=1)
  lane = lax.broadcasted_iota(jnp.int32, filt.shape, 1)
  filt = jnp.where(lane == half, 0.0, filt * jnp.exp(-tn_ref[...] * dl_ref[...]))
  o_ref[...] = filt * lax.rsqrt(jnp.sum(filt * filt, axis=-1, keepdims=True) + EPS)


def hy_filter(hdn_hi, hdn_lo, w4, tn_ext):
  n = hdn_hi.shape[1]
  c = w4.shape[1] // (2 * HY_ORDER)
  tc = 64
  w4t = w4.T.reshape(2, HY_ORDER, c, HY_HID)
  w4h = w4t.astype(BF16)
  w4l = (w4t - w4h.astype(F32)).astype(BF16)
  deltas = jnp.abs(jnp.linspace(math.log(HY_DECAY_TARGET) / HY_FAST_DECAY,
                                math.log(HY_DECAY_TARGET) / HY_SLOW_DECAY, c, dtype=F32)).reshape(c, 1)
  wspec = lambda d: pl.BlockSpec((None, None, tc, HY_HID), lambda o, i: (d, o, i, 0))
  hspec = pl.BlockSpec((HY_HID, n), lambda o, i: (0, 0))
  nct = c // tc
  out_spec = pl.BlockSpec((tc, n), lambda o, i: (o * nct + i, 0))
  out_shape = jax.ShapeDtypeStruct((HY_ORDER * c, n), F32)
  return pl.pallas_call(
      _hy_filter_kernel,
      grid=(HY_ORDER, nct),
      in_specs=[hspec, hspec, wspec(0), wspec(0), wspec(1), wspec(1),
                pl.BlockSpec((1, n), lambda o, i: (0, 0)),
                pl.BlockSpec((tc, 1), lambda o, i: (i, 0))],
      out_specs=out_spec,
      out_shape=out_shape,
      compiler_params=_cparams(("parallel", "parallel")),
      name="hy_filter",
  )(hdn_hi, hdn_lo, w4h, w4l, w4h, w4l, tn_ext, deltas)


def _short_conv_cm(x, cw, period):
  width = x.shape[1]
  assert period & (period - 1) == 0
  t = jnp.bitwise_and(lax.broadcasted_iota(jnp.int32, x.shape, 1), period - 1)
  prev = jnp.where(t == 0, 0.0, pltpu.roll(x, 1, 1))
  nxt = jnp.where(t == period - 1, 0.0, pltpu.roll(x, width - 1, 1))
  return cw[:, 3:4] + cw[:, 0:1] * prev + cw[:, 1:2] * x + cw[:, 2:3] * nxt


def _dft_stage_a(x, fa, tc, ts):
  a = jnp.dot(fa, x.astype(BF16), preferred_element_type=F32)
  ar, ai = a[:DFT_R], a[DFT_R:]
  return jnp.concatenate([ar * tc + ai * ts, ai * tc - ar * ts], axis=1).astype(BF16)


def _hyena_lat_kernel(skip_ref, g0_ref, g1_ref, z_ref, cw0_ref, cw1_ref, cwz_ref, k0_ref, k1_ref,
                      fah_ref, faf_ref, tw_ref, gc_ref, gci_ref, fai_ref, o_ref, h0_ref, h1_ref,
                      *, n_ch):
  i = pl.program_id(0)
  b = pl.program_id(1)
  tc, ts = tw_ref[0], tw_ref[1]
  rows = HY_CG * DFT_R
  half = SEQ // DFT_R
  lane = lax.broadcasted_iota(jnp.int32, (half, DFT_R), 1)
  row = lax.broadcasted_iota(jnp.int32, (half, DFT_R), 0)
  first_lane, last_lane = lane == 0, lane == DFT_R - 1
  first_row, last_row = row == 0, row == half - 1

  def short_conv(x, cw):
    back = pltpu.roll(x, 1, 1)
    wrap = jnp.where(first_row, 0.0, pltpu.roll(back, 1, 0))
    prev = jnp.where(first_lane, wrap, back)
    fwd = pltpu.roll(x, DFT_R - 1, 1)
    wrap = jnp.where(last_row, 0.0, pltpu.roll(fwd, half - 1, 0))
    nxt = jnp.where(last_lane, wrap, fwd)
    return cw[3:4] + cw[0:1] * prev + cw[1:2] * x + cw[2:3] * nxt

  @pl.when(b == 0)
  def _():
    def spectrum(g, carry):
      base = g * HY_CG
      for k_ref, h_ref in ((k0_ref, h0_ref), (k1_ref, h1_ref)):
        ap = jnp.concatenate([_dft_stage_a(k_ref[base + c], faf_ref[...], tc, ts)
                              for c in range(HY_CG)], axis=0)
        h_ref[pl.ds(pl.multiple_of(g * rows, rows), rows), :] = jnp.dot(
            ap, gc_ref[...], preferred_element_type=F32)
      return carry
    lax.fori_loop(0, HY_CT // HY_CG, spectrum, 0)

  def group(g, carry):
    base = g * HY_CG
    zs = [short_conv(z_ref[base + c], cwz_ref[base + c]) for c in range(HY_CG)]
    for o, (h_ref, gate_ref, cw_ref) in enumerate(((h0_ref, g0_ref, cw0_ref),
                                                   (h1_ref, g1_ref, cw1_ref))):
      ap = jnp.concatenate([_dft_stage_a(zs[c], fah_ref[...], tc, ts) for c in range(HY_CG)], axis=0)
      x = jnp.dot(ap, gc_ref[...], preferred_element_type=F32)
      h = h_ref[pl.ds(pl.multiple_of(g * rows, rows), rows), :]
      xr, xi, hr, hi = x[:, :DFT_R], x[:, DFT_R:], h[:, :DFT_R], h[:, DFT_R:]
      y = jnp.concatenate([xr * hr - xi * hi, xr * hi + xi * hr], axis=1).astype(BF16)
      bb = jnp.dot(y, gci_ref[...], preferred_element_type=F32)
      nxt = []
      for c in range(HY_CG):
        bc = bb[c * DFT_R:(c + 1) * DFT_R]
        br, bi = bc[:, :DFT_R], bc[:, DFT_R:]
        st = jnp.concatenate([br * tc - bi * ts, br * ts + bi * tc], axis=0).astype(BF16)
        conv = jnp.dot(fai_ref[...], st, preferred_element_type=F32)
        sk = skip_ref[o * n_ch + i * HY_CT + base + c]
        gate = short_conv(gate_ref[base + c], cw_ref[base + c])
        nxt.append(gate * (conv + sk * zs[c]))
      zs = nxt
    for c in range(HY_CG):
      o_ref[base + c] = zs[c].astype(o_ref.dtype)
    return carry
  lax.fori_loop(0, HY_CT // HY_CG, group, 0)


def hyena_lat(hy_t, cw, kern_t, skip, consts):
  c = hy_t.shape[0] // 3
  nct = c // HY_CT
  half = SEQ // DFT_R
  u3 = hy_t.reshape(3 * c, T_PAD // DFT_R, DFT_R)
  k3 = kern_t.reshape(HY_ORDER * c, DFT_R, DFT_R)
  cwx = jnp.broadcast_to(cw[:, :, None], (3 * c, 4, DFT_R))
  uspec = lambda part: pl.BlockSpec((HY_CT, half, DFT_R), lambda i, b, sk: (part * nct + i, b, 0))
  wspec = lambda part: pl.BlockSpec((HY_CT, 4, DFT_R), lambda i, b, sk: (part * nct + i, 0, 0))
  kspec = lambda o: pl.BlockSpec((HY_CT, DFT_R, DFT_R), lambda i, b, sk: (o * nct + i, 0, 0))
  cspec = lambda a: pl.BlockSpec(a.shape, lambda i, b, sk: (0,) * a.ndim)
  cs = [consts[k] for k in ("fa_half", "fa_full", "tw", "gc", "gci", "fai")]
  out = pl.pallas_call(
      functools.partial(_hyena_lat_kernel, n_ch=c),
      grid_spec=pltpu.PrefetchScalarGridSpec(
          num_scalar_prefetch=1,
          grid=(nct, BATCH),
          in_specs=[uspec(0), uspec(1), uspec(2), wspec(0), wspec(1), wspec(2), kspec(0), kspec(1)]
          + [cspec(a) for a in cs],
          out_specs=pl.BlockSpec((HY_CT, half, DFT_R), lambda i, b, sk: (i, b, 0)),
          scratch_shapes=[pltpu.VMEM((HY_CT * DFT_R, 2 * DFT_R), F32)] * 2,
      ),
      out_shape=jax.ShapeDtypeStruct((c, N_LAT // DFT_R, DFT_R), BF16),
      compiler_params=_cparams(("parallel", "arbitrary")),
      name="hyena_lat",
  )(skip.reshape(-1), u3, u3, u3, cwx, cwx, cwx, k3, k3, *cs)
  return out.reshape(c, N_LAT)


def _hyena_ctx_kernel(g0_ref, g1_ref, z_ref, cw0_ref, cw1_ref, cwz_ref, k0_ref, k1_ref, sk_ref,
                      fwd_ref, fwdh_ref, inv_ref, o_ref):
  n = HY_N_CTX
  g0 = _short_conv_cm(g0_ref[...], cw0_ref[...], CTX_LEN)
  g1 = _short_conv_cm(g1_ref[...], cw1_ref[...], CTX_LEN)
  z = _short_conv_cm(z_ref[...], cwz_ref[...], CTX_LEN)
  sk = sk_ref[...]
  spec = [jnp.dot(k_ref[...].astype(BF16), fwd_ref[...], preferred_element_type=F32)
          for k_ref in (k0_ref, k1_ref)]
  for b in range(BATCH):
    cols = slice(b * CTX_LEN, (b + 1) * CTX_LEN)
    zb = z[:, cols]
    for o, gate in enumerate((g0, g1)):
      x = jnp.dot(zb.astype(BF16), fwdh_ref[...], preferred_element_type=F32)
      h = spec[o]
      xr, xi, hr, hi = x[:, :n], x[:, n:], h[:, :n], h[:, n:]
      y = jnp.concatenate([xr * hr - xi * hi, xr * hi + xi * hr], axis=1).astype(BF16)
      conv = jnp.dot(y, inv_ref[...], preferred_element_type=F32)
      zb = gate[:, cols] * (conv + sk[:, o:o + 1] * zb)
    o_ref[:, cols] = zb.astype(o_ref.dtype)
  o_ref[:, N_CTX:] = jnp.zeros((o_ref.shape[0], TAIL_ROWS - N_CTX), o_ref.dtype)


def hyena_ctx(hy_t, cw, kern_t, skip, consts):
  c = hy_t.shape[0] // 3
  tc = 256 if c % 256 == 0 else c
  nct = c // tc
  blk = CTX_ROW0 // N_CTX
  xspec = lambda part: pl.BlockSpec((tc, N_CTX), lambda i: (part * nct + i, blk))
  wspec = lambda part: pl.BlockSpec((tc, 4), lambda i: (part * nct + i, 0))
  kspec = lambda o: pl.BlockSpec((tc, HY_N_CTX), lambda i: (o * nct + i, 0))
  cspec = lambda a: pl.BlockSpec(a.shape, lambda i: (0,) * a.ndim)
  cs = [consts[k] for k in ("fwd", "fwd_half", "inv")]
  return pl.pallas_call(
      _hyena_ctx_kernel,
      grid=(nct,),
      in_specs=[xspec(0), xspec(1), xspec(2), wspec(0), wspec(1), wspec(2), kspec(0), kspec(1),
                pl.BlockSpec((tc, HY_ORDER), lambda i: (i, 0))] + [cspec(a) for a in cs],
      out_specs=pl.BlockSpec((tc, TAIL_ROWS), lambda i: (i, 0)),
      out_shape=jax.ShapeDtypeStruct((c, TAIL_ROWS), BF16),
      compiler_params=_cparams(("parallel",)),
      name="hyena_ctx",
  )(hy_t, hy_t, hy_t, cw, cw, cw, kern_t, kern_t, skip.T, *cs)


def hyena_cm(hy_t, conv_w, conv_b, w1, b1, w2, b2, w3, b3, w4, freq, skip):
  cw = jnp.concatenate([conv_w.T, conv_b[:, None]], axis=1)
  outs = []
  for length, lat in ((SEQ, True), (CTX_LEN, False)):
    feats_t, tn_ext = _hy_positions(length)
    hdn_hi, hdn_lo = hy_mlp(feats_t, w1, b1, w2, b2, w3, b3, freq)
    kern_t = hy_filter(hdn_hi, hdn_lo, w4, tn_ext)
    if lat:
      outs.append(hyena_lat(hy_t, cw, kern_t, skip, _dft_consts_lat()))
    else:
      outs.append(hyena_ctx(hy_t, cw, kern_t, skip, _dft_consts_ctx()))
  return tuple(outs)


def _mm_nt_kernel(x_ref, w_ref, o_ref, wt_ref):
  @pl.when(pl.program_id(1) == 0)
  def _():
    wt_ref[...] = w_ref[...].T.astype(BF16)

  o_ref[...] = lax.dot_general(wt_ref[...], x_ref[...], (((1,), (1,)), ((), ())),
                               preferred_element_type=F32)


def mm_nt(x, w, layer, col0, n, tn=512):
  k = w.shape[1]
  wspec = pl.BlockSpec((None, k, tn), lambda j, i: (layer, 0, col0 // tn + j),
                       pipeline_mode=pl.Buffered(1))
  return pl.pallas_call(
      _mm_nt_kernel,
      grid=(n // tn, T_PAD // TM),
      in_specs=[_xspec(k), wspec],
      out_specs=pl.BlockSpec((tn, TM), lambda j, i: (j, i)),
      out_shape=jax.ShapeDtypeStruct((n, T_PAD), F32),
      scratch_shapes=[pltpu.VMEM((tn, k), BF16)],
      compiler_params=_cparams(("parallel", "arbitrary")),
      name="mm_nt",
  )(x, w)


def _mm_resid2_kernel(al_ref, ac_ref, zl_ref, zc_ref, wa_ref, wz_ref, s_ref, m_ref, o_ref,
                      *, n_lat_tiles):
  i = pl.program_id(1)

  def mix(a_ref, zt_ref):
    acc = jnp.dot(a_ref[...], wa_ref[...], preferred_element_type=F32)
    acc += lax.dot_general(zt_ref[...], wz_ref[...], (((0,), (0,)), ((), ())),
                           preferred_element_type=F32)
    o_ref[...] = s_ref[...] + m_ref[...] * acc

  @pl.when(i < n_lat_tiles)
  def _():
    mix(al_ref, zl_ref)

  @pl.when(i >= n_lat_tiles)
  def _():
    mix(ac_ref, zc_ref)


def mm_resid2(a_lat, a_tail, z_lat_t, z_tail_t, w, layer, s, gate8, tn=512):
  ka, kz = a_lat.shape[1], z_lat_t.shape[0]
  n = w.shape[2]
  n_lat_tiles = N_LAT // TM
  assert ka == kz and a_tail.shape[0] == TM and z_tail_t.shape[1] == TM
  row = _mod_row(TM)
  lat = lambda i: jnp.minimum(i, n_lat_tiles - 1)
  once = pl.Buffered(1)
  return pl.pallas_call(
      functools.partial(_mm_resid2_kernel, n_lat_tiles=n_lat_tiles),
      grid=(n // tn, T_PAD // TM),
      in_specs=[
          pl.BlockSpec((TM, ka), lambda j, i: (lat(i), 0)),
          pl.BlockSpec((TM, ka), lambda j, i: (0, 0), pipeline_mode=once),
          pl.BlockSpec((kz, TM), lambda j, i: (0, lat(i))),
          pl.BlockSpec((kz, TM), lambda j, i: (0, 0), pipeline_mode=once),
          pl.BlockSpec((None, ka, tn), lambda j, i: (layer, 0, j)),
          pl.BlockSpec((None, kz, tn), lambda j, i: (layer, 1, j)),
          pl.BlockSpec((TM, tn), lambda j, i: (i, j)),
          pl.BlockSpec((None, 1, tn), lambda j, i: (row(i), 0, j)),
      ],
      out_specs=pl.BlockSpec((TM, tn), lambda j, i: (i, j)),
      out_shape=jax.ShapeDtypeStruct((T_PAD, n), F32),
      input_output_aliases={6: 0},
      compiler_params=_cparams(("parallel", "parallel")),
      name="mm_resid2",
  )(a_lat, a_tail, z_lat_t, z_tail_t, w, w, s, gate8)


def kernel(x, c, ctx, c_ctx, ada_a, ada_b, ada_bias, norm_mix, norm_ffn, even_w_in, na_q_gain,
           na_k_gain, na_rpb, hy_conv_w, hy_conv_b, hy_w1, hy_b1, hy_w2, hy_b2, hy_w3, hy_b3,
           hy_w4, hy_freq, hy_skip, odd_w_in, odd_b_in, gm_v_gain, gm_w_s, gm_b_s, w_mix_out,
           moe_router, moe_bias, moe_w_gate, moe_w_up, moe_w_down, sh_w_gate, sh_w_up, sh_w_down):
  s = jnp.concatenate([x.reshape(N_LAT, D_MODEL), ctx.reshape(N_CTX, D_MODEL),
                       jnp.zeros((T_PAD - T_REAL, D_MODEL), F32)], axis=0)
  cond8 = jnp.concatenate([c, c_ctx[None], jnp.zeros((8 - BATCH - 1, D_MODEL), F32)], axis=0)
  mods = ada_table(cond8, ada_a, ada_b, ada_bias).reshape(DEPTH, 8, N_MOD, 1, D_MODEL)
  w_qkv = even_w_in.astype(BF16)
  w_odd = odd_w_in.astype(BF16)
  w_out = w_mix_out.astype(BF16)
  wsg, wsu = sh_w_gate.astype(BF16), sh_w_up.astype(BF16)
  wd_r = moe_w_down.reshape(DEPTH, N_EXPERTS * EXPERT_DIM, D_MODEL).astype(BF16)
  wd_s = sh_w_down.astype(BF16)

  for layer in range(DEPTH):
    j = layer // 2
    rows = N_LAT if layer == DEPTH - 1 else T_PAD
    mod = [mods[layer, :, i] for i in range(N_MOD)]
    if layer % 2 == 0:
      h = norm_mod(s, norm_mix[layer], mod[0], mod[1])
      gains = jnp.concatenate([jnp.tile(na_q_gain[j], NA_HEADS), jnp.tile(na_k_gain[j], NA_HEADS),
                               jnp.ones((NA_WIDTH,), F32)]).reshape(1, 3 * NA_WIDTH)
      qkv = mm_headnorm(h, w_qkv, j, gains, 3 * NA_WIDTH, 2 * NA_WIDTH)
      hy_t = mm_nt(h, even_w_in, j, 3 * NA_WIDTH, 3 * HY_WIDTH)
      a_lat = na_attention(qkv, na_bias(na_rpb[j]))
      a_tail = ctx_attention(qkv)
      z_lat_t, z_tail_t = hyena_cm(hy_t, hy_conv_w[j], hy_conv_b[j], hy_w1[j], hy_b1[j], hy_w2[j],
                                   hy_b2[j], hy_w3[j], hy_b3[j], hy_w4[j], hy_freq[j], hy_skip[j])
      s = mm_resid2(a_lat, a_tail, z_lat_t, z_tail_t, w_out, layer, s, mod[2])
    else:
      h = norm_mod(s, norm_mix[layer], mod[0], mod[1], rows)
      z = mm_gelu(h, w_odd, j, odd_b_in[j], rows)
      bsx = jnp.repeat(gm_b_s[j].T, HEAD_DIM, axis=1)
      mix = gmlp_gate(z, gm_v_gain[j], gm_w_s[j].astype(BF16), bsx)
      s = mm_resid(mix, w_out, layer, s, mod[2], rows)

    f, gates = norm_mod_router(s, norm_ffn[layer], mod[3], mod[4], moe_router[layer], moe_bias[layer],
                               rows)
    gates_t = gates[:, :N_EXPERTS].reshape(rows, N_EXPERTS // EXP_PER_TILE, EXP_PER_TILE)
    gates_t = gates_t.transpose(1, 0, 2)
    hid_r = moe_up(f, moe_w_gate, moe_w_up, layer, gates_t, rows)
    hid_s = moe_up_shared(f, wsg, wsu, layer, rows)
    s = moe_down(hid_r, hid_s, wd_r, wd_s, layer, s, mod[5], rows)

  return s.reshape(BATCH, SEQ, D_MODEL)
```
